```python
import jax, jax.numpy as jnp
from jax import lax
import numpy as np

D_MODEL = 1024
BATCH = 4
SEQ = 4096
DEPTH = 2

CHUNK = 64
BRANCH_WIDTH = D_MODEL // 2
N_BRANCHES = 3
SB_HEADS = 8
SB_HEAD_DIM = BRANCH_WIDTH // SB_HEADS
SB_BLOCK = 128
SGU_LEN = 128
SGU_GROUPS = 4
SGU_GROUP_DIM = BRANCH_WIDTH // SGU_GROUPS
CONV_WIDTH = 3
MEM_TOKENS = 256
XA_HEADS = 4
XA_HEAD_DIM = D_MODEL // XA_HEADS
FFN_HIDDEN = ((8 * D_MODEL // 3 + 255) // 256) * 256

W_QKV = 3 * BRANCH_WIDTH
W_SGU = 2 * BRANCH_WIDTH
W_CONV = 3 * BRANCH_WIDTH
W_GATES = N_BRANCHES * D_MODEL
IN_COLS = W_QKV + W_SGU + W_CONV + W_GATES
SPLIT_IDX = [BRANCH_WIDTH, 2 * BRANCH_WIDTH, W_QKV,
             W_QKV + W_SGU,
             W_QKV + W_SGU + BRANCH_WIDTH, W_QKV + W_SGU + 2 * BRANCH_WIDTH,
             W_QKV + W_SGU + W_CONV]

kernel_name = "hybrid_stickbreak_gmlp_shortconv_block"


def rms_norm(x, g, eps=1e-6):
    xf = x.astype(jnp.float32)
    y = xf * lax.rsqrt(jnp.mean(xf * xf, axis=-1, keepdims=True) + eps)
    return (y * g.astype(jnp.float32)).astype(x.dtype)


def layer_norm(x, g, b, eps=1e-5):
    xf = x.astype(jnp.float32)
    mu = jnp.mean(xf, axis=-1, keepdims=True)
    xc = xf - mu
    y = xc * lax.rsqrt(jnp.mean(xc * xc, axis=-1, keepdims=True) + eps)
    return (y * g.astype(jnp.float32) + b.astype(jnp.float32)).astype(x.dtype)


def stick_breaking_attention(q, k, v):
    seq = q.shape[2]
    scale = SB_HEAD_DIM ** -0.5
    outs = []
    for i in range(seq // SB_BLOCK):
        q0 = i * SB_BLOCK
        kend = q0 + SB_BLOCK
        qb = q[:, :, q0:kend].astype(jnp.float32)
        kb = k[:, :, :kend].astype(jnp.float32)
        z = jnp.einsum("bhqd,bhkd->bhqk", qb, kb) * scale
        t_pos = q0 + jnp.arange(SB_BLOCK)[:, None]
        s_pos = jnp.arange(kend)[None, :]
        valid = s_pos < t_pos
        log_1m = jnp.where(valid, jax.nn.log_sigmoid(-z), 0.0)
        log_a = jax.nn.log_sigmoid(z) + lax.cumsum(log_1m, axis=3, reverse=True) - log_1m
        a = jnp.where(valid, jnp.exp(log_a), 0.0)
        outs.append(jnp.einsum("bhqk,bhkd->bhqd", a.astype(v.dtype), v[:, :, :kend]))
    return jnp.concatenate(outs, axis=2)


def spatial_gating(z, ln_g, ln_b, w_s, b_s):
    bsz, seq, _ = z.shape
    u, v = jnp.split(z, 2, axis=-1)
    v = layer_norm(v, ln_g, ln_b)
    v = v.reshape(bsz, seq // SGU_LEN, SGU_LEN, SGU_GROUPS, SGU_GROUP_DIM)
    pos = jnp.arange(SGU_LEN)
    allowed = (pos[:, None] // CHUNK) >= (pos[None, :] // CHUNK)
    w = jnp.where(allowed[None], w_s, 0.0).astype(v.dtype)
    vm = jnp.einsum("gts,bnsgc->bntgc", w, v) + b_s.T[None, None, :, :, None].astype(v.dtype)
    return u * vm.reshape(bsz, seq, BRANCH_WIDTH)


def gated_short_conv(gate_b, gate_c, xin, conv_w):
    y = gate_c * xin
    ch = y.shape[-1]
    conv = lax.conv_general_dilated(
        y, conv_w[:, None, :].astype(y.dtype), window_strides=(1,),
        padding=((CONV_WIDTH - 1, 0),), dimension_numbers=("NWC", "WIO", "NWC"),
        feature_group_count=ch)
    return gate_b * conv


def hybrid_mixer(h, w_in, sgu_ln_g, sgu_ln_b, w_spatial, b_spatial, conv_w, w_branch, w_out):
    bsz, seq, _ = h.shape
    p = h @ w_in
    q, k, v, z, cb, cc, cx, gates = jnp.split(p, SPLIT_IDX, axis=-1)

    def heads(t):
        return t.reshape(bsz, seq, SB_HEADS, SB_HEAD_DIM).transpose(0, 2, 1, 3)

    ya = stick_breaking_attention(heads(q), heads(k), heads(v))
    ya = ya.transpose(0, 2, 1, 3).reshape(bsz, seq, BRANCH_WIDTH)
    yb = spatial_gating(jax.nn.gelu(z, approximate=False), sgu_ln_g, sgu_ln_b, w_spatial, b_spatial)
    yc = gated_short_conv(cb, cc, cx, conv_w)

    br = jnp.stack([ya, yb, yc], axis=2)
    br_d = jnp.einsum("bsnc,ncd->bsnd", br, w_branch)
    g = jax.nn.sigmoid(gates.reshape(bsz, seq, N_BRANCHES, D_MODEL))
    merged = jnp.sum(g * br_d, axis=2)
    return merged @ w_out


def memory_cross_attention(h, mem, mem_g, wq, wk, wv, wo):
    bsz, seq, _ = h.shape
    m = rms_norm(mem, mem_g)
    q = (h @ wq).reshape(bsz, seq, XA_HEADS, XA_HEAD_DIM)
    k = (m @ wk).reshape(bsz, MEM_TOKENS, XA_HEADS, XA_HEAD_DIM)
    v = (m @ wv).reshape(bsz, MEM_TOKENS, XA_HEADS, XA_HEAD_DIM)
    s = jnp.einsum("bqhd,bkhd->bhqk", q.astype(jnp.float32), k.astype(jnp.float32)) * (XA_HEAD_DIM ** -0.5)
    pr = jax.nn.softmax(s, axis=-1)
    o = jnp.einsum("bhqk,bkhd->bqhd", pr.astype(v.dtype), v).reshape(bsz, seq, D_MODEL)
    return o @ wo


def swiglu(h, w_gate, w_up, w_down):
    return (jax.nn.silu(h @ w_gate) * (h @ w_up)) @ w_down


def setup_inputs(seed: int = 0) -> dict:
    key = jax.random.key(seed)
    ks = jax.random.split(key, 24)
    f32 = jnp.float32

    def nrm(k, shape, fan_in):
        return jax.random.normal(k, shape, f32) * (fan_in ** -0.5)

    def gain(k, shape):
        return 1.0 + 0.02 * jax.random.normal(k, shape, f32)

    L, D, W = DEPTH, D_MODEL, BRANCH_WIDTH
    return {
        "x": jax.random.normal(ks[0], (BATCH, SEQ, D), f32),
        "mem": jax.random.normal(ks[1], (BATCH, MEM_TOKENS, D), f32),
        "norm_mix_g": gain(ks[2], (L, D)),
        "w_in": nrm(ks[3], (L, D, IN_COLS), D),
        "sgu_ln_g": gain(ks[4], (L, W)),
        "sgu_ln_b": 0.02 * jax.random.normal(ks[5], (L, W), f32),
        "w_spatial": nrm(ks[6], (L, SGU_GROUPS, SGU_LEN, SGU_LEN), SGU_LEN),
        "b_spatial": gain(ks[7], (L, SGU_GROUPS, SGU_LEN)),
        "conv_w": nrm(ks[8], (L, CONV_WIDTH, W), CONV_WIDTH),
        "w_branch": nrm(ks[9], (L, N_BRANCHES, W, D), W),
        "w_out": nrm(ks[10], (L, D, D), D),
        "norm_xa_g": gain(ks[11], (L, D)),
        "mem_norm_g": gain(ks[12], (L, D)),
        "w_q_xa": nrm(ks[13], (L, D, D), D),
        "w_k_xa": nrm(ks[14], (L, D, D), D),
        "w_v_xa": nrm(ks[15], (L, D, D), D),
        "w_o_xa": nrm(ks[16], (L, D, D), D),
        "norm_ffn_g": gain(ks[17], (L, D)),
        "w_gate_ffn": nrm(ks[18], (L, D, FFN_HIDDEN), D),
        "w_up_ffn": nrm(ks[19], (L, D, FFN_HIDDEN), D),
        "w_down_ffn": nrm(ks[20], (L, FFN_HIDDEN, D), FFN_HIDDEN),
        "final_g": gain(ks[21], (D,)),
    }


def reference(x, mem, norm_mix_g, w_in, sgu_ln_g, sgu_ln_b, w_spatial, b_spatial, conv_w,
              w_branch, w_out, norm_xa_g, mem_norm_g, w_q_xa, w_k_xa, w_v_xa, w_o_xa,
              norm_ffn_g, w_gate_ffn, w_up_ffn, w_down_ffn, final_g):
    for l in range(DEPTH):
        x = x + hybrid_mixer(rms_norm(x, norm_mix_g[l]), w_in[l], sgu_ln_g[l], sgu_ln_b[l],
                             w_spatial[l], b_spatial[l], conv_w[l], w_branch[l], w_out[l])
        x = x + memory_cross_attention(rms_norm(x, norm_xa_g[l]), mem, mem_norm_g[l],
                                       w_q_xa[l], w_k_xa[l], w_v_xa[l], w_o_xa[l])
        x = x + swiglu(rms_norm(x, norm_ffn_g[l]), w_gate_ffn[l], w_up_ffn[l], w_down_ffn[l])
    return rms_norm(x, final_g)
```

```python
import functools

import jax
import jax.numpy as jnp
from jax import lax
from jax.experimental import pallas as pl
from jax.experimental.pallas import tpu as pltpu

F32 = jnp.float32
BF16 = jnp.bfloat16

LANES = 128
V7X_VMEM_LIMIT_BYTES = 56 * 1024 * 1024

SB_HEADS = 8
SB_HEAD_DIM = 64
HEADS_PER_LANE_TILE = LANES // SB_HEAD_DIM
SGU_LEN = 128
SGU_GROUPS = 4
CHUNK = 64
CONV_WIDTH = 3
XA_HEADS = 4
RMS_EPS = 1e-6
LN_EPS = 1e-5

PROJ_TM = 1024
PROJ_TN = 512
ATT_T = 128
SGU_TM = 512
CONV_TM = 512
MERGE_TM = 512
XA_TM = 512
FFN_TM = 1024
FFN_TH = 256
CONV_HALO = 8


def _params(*semantics):
    return pltpu.CompilerParams(dimension_semantics=semantics,
                                vmem_limit_bytes=V7X_VMEM_LIMIT_BYTES)


def _rms(x, g):
    return x * lax.rsqrt(jnp.mean(x * x, axis=-1, keepdims=True) + RMS_EPS) * g


def _dot(a, b):
    return jnp.dot(a, b, preferred_element_type=F32)


def _dot_nt(a, b):
    return lax.dot_general(a, b, (((1,), (1,)), ((), ())), preferred_element_type=F32)


def _norm_matmul_kernel(x_ref, g_ref, w_ref, o_ref, h_ref):
    @pl.when(pl.program_id(1) == 0)
    def _():
        h_ref[...] = _rms(x_ref[...], g_ref[...]).astype(BF16)

    o_ref[...] = _dot(h_ref[...], w_ref[...]).astype(o_ref.dtype)


def _norm_matmul(x, g, w, name):
    rows, d = x.shape
    cols = w.shape[1]
    tm = min(PROJ_TM, rows)
    return pl.pallas_call(
        _norm_matmul_kernel,
        out_shape=jax.ShapeDtypeStruct((rows, cols), BF16),
        grid=(rows // tm, cols // PROJ_TN),
        in_specs=[
            pl.BlockSpec((tm, d), lambda i, j: (i, 0)),
            pl.BlockSpec((1, d), lambda i, j: (0, 0)),
            pl.BlockSpec((d, PROJ_TN), lambda i, j: (0, j)),
        ],
        out_specs=pl.BlockSpec((tm, PROJ_TN), lambda i, j: (i, j)),
        scratch_shapes=[pltpu.VMEM((tm, d), BF16)],
        compiler_params=_params("parallel", "arbitrary"),
        name=name,
    )(x, g, w)


def _sb_attn_kernel(q_ref, k_ref, v_ref, tri_ref, o_ref, acc_ref, carry_ref):
    t = ATT_T
    i = pl.program_id(2)
    lane = lax.broadcasted_iota(jnp.int32, (t, LANES), 1)
    q = q_ref[...] * jnp.asarray(SB_HEAD_DIM ** -0.5, BF16)
    zero = jnp.zeros_like(q)
    q2 = jnp.concatenate([jnp.where(lane < SB_HEAD_DIM, q, zero),
                          jnp.where(lane >= SB_HEAD_DIM, q, zero)], axis=0)

    def tile(j, mask):
        start = pl.multiple_of(j * t, t)
        ks = k_ref[pl.ds(start, t), :]
        vs = v_ref[pl.ds(start, t), :]
        z = _dot_nt(q2, ks)
        sp = jnp.log(1.0 + jnp.exp(-jnp.abs(z)))
        log_sig = jnp.minimum(z, 0.0) - sp
        log_1m = -jnp.maximum(z, 0.0) - sp
        if mask is not None:
            log_1m = jnp.where(mask, log_1m, 0.0)
        hi = log_1m.astype(BF16)
        lo = (log_1m - hi.astype(F32)).astype(BF16)
        cs = _dot(jnp.concatenate([hi, lo], axis=1), tri_ref[...])
        a = jnp.exp(log_sig + cs[:, :t] + carry_ref[...])
        if mask is not None:
            a = jnp.where(mask, a, 0.0)
        acc_ref[...] += _dot(a.astype(BF16), vs)
        carry_ref[...] += cs[:, t:]

    acc_ref[...] = jnp.zeros_like(acc_ref)
    carry_ref[...] = jnp.zeros_like(carry_ref)
    row = lax.broadcasted_iota(jnp.int32, (2 * t, t), 0)
    col = lax.broadcasted_iota(jnp.int32, (2 * t, t), 1)
    tile(i, col < jnp.where(row >= t, row - t, row))

    def body(n, c):
        tile(i - 1 - n, None)
        return c

    lax.fori_loop(0, i, body, 0)
    acc = acc_ref[...]
    o_ref[...] = jnp.where(lane < SB_HEAD_DIM, acc[:t], acc[t:]).astype(o_ref.dtype)


def _tri_matrix():
    t = ATT_T
    j = lax.broadcasted_iota(jnp.int32, (t, t), 0)
    s = lax.broadcasted_iota(jnp.int32, (t, t), 1)
    half = jnp.concatenate([(j > s).astype(BF16), jnp.ones((t, t), BF16)], axis=1)
    return jnp.concatenate([half, half], axis=0)


def _sb_attention(qkv, name):
    b, s, _ = qkv.shape
    t = ATT_T
    width = SB_HEADS * SB_HEAD_DIM
    pairs = width // LANES
    return pl.pallas_call(
        _sb_attn_kernel,
        out_shape=jax.ShapeDtypeStruct((b, s, width), BF16),
        grid=(b, pairs, s // t),
        in_specs=[
            pl.BlockSpec((None, t, LANES), lambda bi, p, i: (bi, i, p)),
            pl.BlockSpec((None, s, LANES), lambda bi, p, i: (bi, 0, pairs + p)),
            pl.BlockSpec((None, s, LANES), lambda bi, p, i: (bi, 0, 2 * pairs + p)),
            pl.BlockSpec((2 * t, 2 * t), lambda bi, p, i: (0, 0)),
        ],
        out_specs=pl.BlockSpec((None, t, LANES), lambda bi, p, i: (bi, i, p)),
        scratch_shapes=[pltpu.VMEM((2 * t, LANES), F32), pltpu.VMEM((2 * t, t), F32)],
        compiler_params=_params("parallel", "parallel", "arbitrary"),
        name=name,
    )(qkv, qkv, qkv, _tri_matrix())


def _sgu_kernel(x_ref, g_ref, w_ref, lng_ref, lnb_ref, ws_ref, bs_ref, o_ref):
    tm = x_ref.shape[0]
    wd = o_ref.shape[1]
    gd = wd // SGU_GROUPS
    h = _rms(x_ref[...], g_ref[...]).astype(BF16)
    z = _dot(h, w_ref[...])
    z = 0.5 * z * (1.0 + lax.erf(z * (2.0 ** -0.5)))
    u = z[:, :wd]
    v = z[:, wd:]
    mu = jnp.mean(v, axis=-1, keepdims=True)
    vc = v - mu
    v = vc * lax.rsqrt(jnp.mean(vc * vc, axis=-1, keepdims=True) + LN_EPS)
    v = (v * lng_ref[...] + lnb_ref[...]).astype(BF16)
    pos_t = lax.broadcasted_iota(jnp.int32, (SGU_LEN, SGU_LEN), 0)
    pos_s = lax.broadcasted_iota(jnp.int32, (SGU_LEN, SGU_LEN), 1)
    allowed = (pos_t // CHUNK) >= (pos_s // CHUNK)
    for gi in range(SGU_GROUPS):
        wm = jnp.where(allowed, ws_ref[gi], 0.0).astype(BF16)
        bias = bs_ref[:, gi:gi + 1]
        for c in range(tm // SGU_LEN):
            rows = slice(c * SGU_LEN, (c + 1) * SGU_LEN)
            cols = slice(gi * gd, (gi + 1) * gd)
            vm = _dot(wm, v[rows, cols]) + bias
            o_ref[rows, cols] = (u[rows, cols] * vm).astype(o_ref.dtype)


def _sgu(x, g, w, ln_g, ln_b, w_s, b_s_t, name):
    rows, d = x.shape
    wd = w.shape[1] // 2
    tm = SGU_TM
    const2 = lambda i: (0, 0)
    return pl.pallas_call(
        _sgu_kernel,
        out_shape=jax.ShapeDtypeStruct((rows, wd), BF16),
        grid=(rows // tm,),
        in_specs=[
            pl.BlockSpec((tm, d), lambda i: (i, 0)),
            pl.BlockSpec((1, d), const2),
            pl.BlockSpec((d, 2 * wd), const2),
            pl.BlockSpec((1, wd), const2),
            pl.BlockSpec((1, wd), const2),
            pl.BlockSpec((SGU_GROUPS, SGU_LEN, SGU_LEN), lambda i: (0, 0, 0)),
            pl.BlockSpec((SGU_LEN, SGU_GROUPS), const2),
        ],
        out_specs=pl.BlockSpec((tm, wd), lambda i: (i, 0)),
        compiler_params=_params("parallel"),
        name=name,
    )(x, g, w, ln_g, ln_b, w_s, b_s_t)


def _conv_kernel(x_ref, g_ref, w_ref, cw_ref, o_ref, halo_ref, *, tiles_per_seq):
    tm = x_ref.shape[0]
    wd = o_ref.shape[1]
    h = _rms(x_ref[...], g_ref[...]).astype(BF16)
    p = _dot(h, w_ref[...])
    cb = p[:, :wd]
    y = p[:, wd:2 * wd] * p[:, 2 * wd:]
    seq_start = pl.program_id(0) % tiles_per_seq == 0
    halo = jnp.where(seq_start, 0.0, halo_ref[...])
    prev1 = halo[CONV_HALO - 1:CONV_HALO, :]
    prev2 = halo[CONV_HALO - 2:CONV_HALO - 1, :]
    row = lax.broadcasted_iota(jnp.int32, (tm, wd), 0)
    y1 = jnp.where(row == 0, prev1, pltpu.roll(y, 1, axis=0))
    y2 = jnp.where(row == 0, prev2, jnp.where(row == 1, prev1, pltpu.roll(y, 2, axis=0)))
    cw = cw_ref[...]
    conv = cw[0:1, :] * y2 + cw[1:2, :] * y1 + cw[2:3, :] * y
    o_ref[...] = (cb * conv).astype(o_ref.dtype)
    halo_ref[...] = y[tm - CONV_HALO:, :]


def _gated_conv(x, g, w, conv_w, seq, name):
    rows, d = x.shape
    wd = w.shape[1] // 3
    tm = CONV_TM
    const2 = lambda i: (0, 0)
    return pl.pallas_call(
        functools.partial(_conv_kernel, tiles_per_seq=seq // tm),
        out_shape=jax.ShapeDtypeStruct((rows, wd), BF16),
        grid=(rows // tm,),
        in_specs=[
            pl.BlockSpec((tm, d), lambda i: (i, 0)),
            pl.BlockSpec((1, d), const2),
            pl.BlockSpec((d, 3 * wd), const2),
            pl.BlockSpec((CONV_WIDTH, wd), const2),
        ],
        out_specs=pl.BlockSpec((tm, wd), lambda i: (i, 0)),
        scratch_shapes=[pltpu.VMEM((CONV_HALO, wd), F32)],
        compiler_params=_params("arbitrary"),
        name=name,
    )(x, g, w, conv_w)


def _merge_kernel(x_ref, g_ref, wg_ref, ya_ref, yb_ref, yc_ref, wb_ref, wo_ref, o_ref):
    d = x_ref.shape[1]
    x = x_ref[...]
    h = _rms(x, g_ref[...]).astype(BF16)
    merged = None
    for n, y_ref in enumerate((ya_ref, yb_ref, yc_ref)):
        gate = jax.nn.sigmoid(_dot(h, wg_ref[:, n * d:(n + 1) * d]))
        term = gate * _dot(y_ref[...], wb_ref[n])
        merged = term if merged is None else merged + term
    o_ref[...] = x + _dot(merged.astype(BF16), wo_ref[...])


def _merge(x, g, w_gates, ya, yb, yc, w_branch, w_out, name):
    rows, d = x.shape
    wd = ya.shape[1]
    tm = MERGE_TM
    const2 = lambda i: (0, 0)
    row_d = pl.BlockSpec((tm, d), lambda i: (i, 0))
    row_w = pl.BlockSpec((tm, wd), lambda i: (i, 0))
    return pl.pallas_call(
        _merge_kernel,
        out_shape=jax.ShapeDtypeStruct((rows, d), F32),
        grid=(rows // tm,),
        in_specs=[
            row_d,
            pl.BlockSpec((1, d), const2),
            pl.BlockSpec((d, 3 * d), const2),
            row_w, row_w, row_w,
            pl.BlockSpec((3, wd, d), lambda i: (0, 0, 0)),
            pl.BlockSpec((d, d), const2),
        ],
        out_specs=row_d,
        compiler_params=_params("parallel"),
        name=name,
    )(x, g, w_gates, ya, yb, yc, w_branch, w_out)


def _xattn_kernel(x_ref, g_ref, wq_ref, k_ref, v_ref, wo_ref, o_ref):
    d = x_ref.shape[1]
    dh = d // XA_HEADS
    x = x_ref[...]
    h = _rms(x, g_ref[...]).astype(BF16)
    q = (_dot(h, wq_ref[...]) * (dh ** -0.5)).astype(BF16)
    heads = []
    for hd in range(XA_HEADS):
        cols = slice(hd * dh, (hd + 1) * dh)
        s = _dot_nt(q[:, cols], k_ref[:, cols])
        p = jnp.exp(s - jnp.max(s, axis=-1, keepdims=True))
        p = p / jnp.sum(p, axis=-1, keepdims=True)
        heads.append(_dot(p.astype(BF16), v_ref[:, cols]).astype(BF16))
    o_ref[...] = x + _dot(jnp.concatenate(heads, axis=1), wo_ref[...])


def _xattn(x, g, wq, kv, wo, name):
    b, s, d = x.shape
    mem = kv.shape[1]
    tm = XA_TM
    const2 = lambda bi, i: (0, 0)
    row_d = pl.BlockSpec((None, tm, d), lambda bi, i: (bi, i, 0))
    return pl.pallas_call(
        _xattn_kernel,
        out_shape=jax.ShapeDtypeStruct((b, s, d), F32),
        grid=(b, s // tm),
        in_specs=[
            row_d,
            pl.BlockSpec((1, d), const2),
            pl.BlockSpec((d, d), const2),
            pl.BlockSpec((None, mem, d), lambda bi, i: (bi, 0, 0)),
            pl.BlockSpec((None, mem, d), lambda bi, i: (bi, 0, 1)),
            pl.BlockSpec((d, d), const2),
        ],
        out_specs=row_d,
        compiler_params=_params("parallel", "parallel"),
        name=name,
    )(x, g, wq, kv, kv, wo)


def _ffn_kernel(x_ref, g_ref, wg_ref, wu_ref, wd_ref, fg_ref, o_ref, h_ref, acc_ref, *, final_norm):
    j = pl.program_id(1)

    @pl.when(j == 0)
    def _():
        h_ref[...] = _rms(x_ref[...], g_ref[...]).astype(BF16)
        acc_ref[...] = jnp.zeros_like(acc_ref)

    h = h_ref[...]
    a = _dot(h, wg_ref[...])
    t = (a * jax.nn.sigmoid(a)) * _dot(h, wu_ref[...])
    acc_ref[...] += _dot(t.astype(BF16), wd_ref[...])

    @pl.when(j == pl.num_programs(1) - 1)
    def _():
        y = x_ref[...] + acc_ref[...]
        if final_norm:
            y = _rms(y, fg_ref[...])
        o_ref[...] = y


def _ffn(x, g, w_gate, w_up, w_down, final_g, final_norm, name):
    rows, d = x.shape
    hid = w_gate.shape[1]
    tm, th = FFN_TM, FFN_TH
    const2 = lambda i, j: (0, 0)
    row_d = pl.BlockSpec((tm, d), lambda i, j: (i, 0))
    return pl.pallas_call(
        functools.partial(_ffn_kernel, final_norm=final_norm),
        out_shape=jax.ShapeDtypeStruct((rows, d), F32),
        grid=(rows // tm, hid // th),
        in_specs=[
            row_d,
            pl.BlockSpec((1, d), const2),
            pl.BlockSpec((d, th), lambda i, j: (0, j)),
            pl.BlockSpec((d, th), lambda i, j: (0, j)),
            pl.BlockSpec((th, d), lambda i, j: (j, 0)),
            pl.BlockSpec((1, d), const2),
        ],
        out_specs=row_d,
        scratch_shapes=[pltpu.VMEM((tm, d), BF16), pltpu.VMEM((tm, d), F32)],
        compiler_params=_params("parallel", "arbitrary"),
        name=name,
    )(x, g, w_gate, w_up, w_down, final_g)


def kernel(x, mem, norm_mix_g, w_in, sgu_ln_g, sgu_ln_b, w_spatial, b_spatial, conv_w, w_branch, w_out, norm_xa_g, mem_norm_g, w_q_xa, w_k_xa, w_v_xa, w_o_xa, norm_ffn_g, w_gate_ffn, w_up_ffn, w_down_ffn, final_g):
    b, s, d = x.shape
    depth = w_in.shape[0]
    wd = w_branch.shape[2]
    n_mem = mem.shape[1]
    assert d % LANES == 0 and s % max(ATT_T, SGU_TM, CONV_TM, MERGE_TM, XA_TM) == 0
    assert (b * s) % max(PROJ_TM, FFN_TM) == 0 and wd == SB_HEADS * SB_HEAD_DIM
    assert w_gate_ffn.shape[2] % FFN_TH == 0 and CONV_TM % SGU_LEN == 0

    c_qkv, c_sgu, c_conv = 3 * wd, 5 * wd, 8 * wd
    row = lambda v: v.reshape(1, -1)
    xf = x.reshape(b * s, d)
    memf = mem.reshape(b * n_mem, d)
    for l in range(depth):
        w_in_l = w_in[l].astype(BF16)
        g_mix = row(norm_mix_g[l])
        qkv = _norm_matmul(xf, g_mix, w_in_l[:, :c_qkv], f"qkv_{l}")
        ya = _sb_attention(qkv.reshape(b, s, c_qkv), f"sb_attn_{l}").reshape(b * s, wd)
        yb = _sgu(xf, g_mix, w_in_l[:, c_qkv:c_sgu], row(sgu_ln_g[l]), row(sgu_ln_b[l]),
                  w_spatial[l], b_spatial[l].T, f"sgu_{l}")
        yc = _gated_conv(xf, g_mix, w_in_l[:, c_sgu:c_conv], conv_w[l], s, f"conv_{l}")
        xf = _merge(xf, g_mix, w_in_l[:, c_conv:], ya, yb, yc, w_branch[l].astype(BF16),
                    w_out[l].astype(BF16), f"merge_{l}")
        w_kv = jnp.concatenate([w_k_xa[l], w_v_xa[l]], axis=1).astype(BF16)
        kv = _norm_matmul(memf, row(mem_norm_g[l]), w_kv, f"mem_kv_{l}").reshape(b, n_mem, 2 * d)
        xf = _xattn(xf.reshape(b, s, d), row(norm_xa_g[l]), w_q_xa[l].astype(BF16), kv,
                    w_o_xa[l].astype(BF16), f"xattn_{l}").reshape(b * s, d)
        xf = _ffn(xf, row(norm_ffn_g[l]), w_gate_ffn[l].astype(BF16), w_up_ffn[l].astype(BF16),
                  w_down_ffn[l].astype(BF16), row(final_g), l == depth - 1, f"ffn_{l}")
    return xf.reshape(b, s, d)
```

```python
import functools

import jax
import jax.numpy as jnp
from jax import lax
from jax.experimental import pallas as pl
from jax.experimental.pallas import tpu as pltpu

F32 = jnp.float32
BF16 = jnp.bfloat16

LANES = 128
V7X_VMEM_LIMIT_BYTES = 56 * 1024 * 1024

SB_HEADS = 8
SB_HEAD_DIM = 64
HEADS_PER_LANE_TILE = LANES // SB_HEAD_DIM
SGU_LEN = 128
SGU_GROUPS = 4
CHUNK = 64
CONV_WIDTH = 3
XA_HEADS = 4
RMS_EPS = 1e-6
LOG2E = 1.4426950408889634
LN_EPS = 1e-5

PROJ_TM = 1024
PROJ_TN = 512
ATT_TQ = 256
ATT_KG = 512
ATT_KB = 256
SGU_TM = 512
CONV_TM = 512
MERGE_TM = 512
XA_TM = 512
FFN_TM = 1024
FFN_TH = 256
CONV_HALO = 8


def _params(*semantics):
    return pltpu.CompilerParams(dimension_semantics=semantics,
                                vmem_limit_bytes=V7X_VMEM_LIMIT_BYTES)


def _rms(x, g):
    return x * lax.rsqrt(jnp.mean(x * x, axis=-1, keepdims=True) + RMS_EPS) * g


def _dot(a, b):
    return jnp.dot(a, b, preferred_element_type=F32)


def _dot_nt(a, b):
    return lax.dot_general(a, b, (((1,), (1,)), ((), ())), preferred_element_type=F32)


def _norm_matmul_kernel(x_ref, g_ref, w_ref, o_ref, h_ref):
    @pl.when(pl.program_id(1) == 0)
    def _():
        h_ref[...] = _rms(x_ref[...], g_ref[...]).astype(BF16)

    o_ref[...] = _dot(h_ref[...], w_ref[...]).astype(o_ref.dtype)


def _norm_matmul(x, g, w, name):
    rows, d = x.shape
    cols = w.shape[1]
    tm = min(PROJ_TM, rows)
    return pl.pallas_call(
        _norm_matmul_kernel,
        out_shape=jax.ShapeDtypeStruct((rows, cols), BF16),
        grid=(rows // tm, cols // PROJ_TN),
        in_specs=[
            pl.BlockSpec((tm, d), lambda i, j: (i, 0)),
            pl.BlockSpec((1, d), lambda i, j: (0, 0)),
            pl.BlockSpec((d, PROJ_TN), lambda i, j: (0, j)),
        ],
        out_specs=pl.BlockSpec((tm, PROJ_TN), lambda i, j: (i, j)),
        scratch_shapes=[pltpu.VMEM((tm, d), BF16)],
        compiler_params=_params("parallel", "arbitrary"),
        name=name,
    )(x, g, w)


def _sb_attn_kernel(q_ref, k_ref, v_ref, tri_ref, o_ref, acc_ref, carry_ref):
    tq, kg, kb = ATT_TQ, ATT_KG, ATT_KB
    i = pl.program_id(2)
    lane = lax.broadcasted_iota(jnp.int32, (tq, LANES), 1)
    q = q_ref[...] * jnp.asarray(SB_HEAD_DIM ** -0.5, BF16)
    zero = jnp.zeros_like(q)
    q2 = jnp.concatenate([jnp.where(lane < SB_HEAD_DIM, q, zero),
                          jnp.where(lane >= SB_HEAD_DIM, q, zero)], axis=0)

    def group(g, mask):
        start = pl.multiple_of(g * kg, kg)
        ks = k_ref[pl.ds(start, kg), :]
        vs = v_ref[pl.ds(start, kg), :]
        z = _dot_nt(q2, ks)
        sp = jnp.log(1.0 + jnp.exp2(jnp.abs(z) * -LOG2E))
        log_sig = jnp.minimum(z, 0.0) - sp
        log_1m = log_sig - z
        if mask is not None:
            log_1m = jnp.where(mask, log_1m, 0.0)
        l16 = log_1m.astype(BF16)
        carry = carry_ref[...]
        n_blocks = kg // kb
        log_a = [None] * n_blocks
        for n in reversed(range(n_blocks)):
            cols = slice(n * kb, (n + 1) * kb)
            log_a[n] = log_sig[:, cols] + _dot(l16[:, cols], tri_ref[...]) + carry
            carry = carry + jnp.sum(log_1m[:, cols], axis=-1, keepdims=True)
        carry_ref[...] = carry
        a = jnp.exp(jnp.concatenate(log_a, axis=1))
        if mask is not None:
            a = jnp.where(mask, a, 0.0)
        acc_ref[...] += _dot(a.astype(BF16), vs)

    acc_ref[...] = jnp.zeros_like(acc_ref)
    carry_ref[...] = jnp.zeros_like(carry_ref)
    n_groups = ((i + 1) * tq + kg - 1) // kg
    row = lax.broadcasted_iota(jnp.int32, (2 * tq, kg), 0)
    col = lax.broadcasted_iota(jnp.int32, (2 * tq, kg), 1)
    t_pos = i * tq + jnp.where(row >= tq, row - tq, row)
    group(n_groups - 1, (n_groups - 1) * kg + col < t_pos)

    def body(n, c):
        group(n_groups - 2 - n, None)
        return c

    lax.fori_loop(0, n_groups - 1, body, 0)
    acc = acc_ref[...]
    o_ref[...] = jnp.where(lane < SB_HEAD_DIM, acc[:tq], acc[tq:]).astype(o_ref.dtype)


def _tri_matrix():
    j = lax.broadcasted_iota(jnp.int32, (ATT_KB, ATT_KB), 0)
    s = lax.broadcasted_iota(jnp.int32, (ATT_KB, ATT_KB), 1)
    return (j > s).astype(BF16)


def _sb_attention(qkv, name):
    b, s, _ = qkv.shape
    tq = ATT_TQ
    width = SB_HEADS * SB_HEAD_DIM
    pairs = width // LANES
    return pl.pallas_call(
        _sb_attn_kernel,
        out_shape=jax.ShapeDtypeStruct((b, s, width), BF16),
        grid=(b, pairs, s // tq),
        in_specs=[
            pl.BlockSpec((None, tq, LANES), lambda bi, p, i: (bi, i, p)),
            pl.BlockSpec((None, s, LANES), lambda bi, p, i: (bi, 0, pairs + p)),
            pl.BlockSpec((None, s, LANES), lambda bi, p, i: (bi, 0, 2 * pairs + p)),
            pl.BlockSpec((ATT_KB, ATT_KB), lambda bi, p, i: (0, 0)),
        ],
        out_specs=pl.BlockSpec((None, tq, LANES), lambda bi, p, i: (bi, i, p)),
        scratch_shapes=[pltpu.VMEM((2 * tq, LANES), F32), pltpu.VMEM((2 * tq, 1), F32)],
        compiler_params=_params("parallel", "parallel", "arbitrary"),
        name=name,
    )(qkv, qkv, qkv, _tri_matrix())


def _sgu_kernel(x_ref, g_ref, w_ref, lng_ref, lnb_ref, ws_ref, bs_ref, o_ref):
    tm = x_ref.shape[0]
    wd = o_ref.shape[1]
    gd = wd // SGU_GROUPS
    h = _rms(x_ref[...], g_ref[...]).astype(BF16)
    z = _dot(h, w_ref[...])
    z = 0.5 * z * (1.0 + lax.erf(z * (2.0 ** -0.5)))
    u = z[:, :wd]
    v = z[:, wd:]
    mu = jnp.mean(v, axis=-1, keepdims=True)
    vc = v - mu
    v = vc * lax.rsqrt(jnp.mean(vc * vc, axis=-1, keepdims=True) + LN_EPS)
    v = (v * lng_ref[...] + lnb_ref[...]).astype(BF16)
    pos_t = lax.broadcasted_iota(jnp.int32, (SGU_LEN, SGU_LEN), 0)
    pos_s = lax.broadcasted_iota(jnp.int32, (SGU_LEN, SGU_LEN), 1)
    allowed = (pos_t // CHUNK) >= (pos_s // CHUNK)
    for gi in range(SGU_GROUPS):
        wm = jnp.where(allowed, ws_ref[gi], 0.0).astype(BF16)
        bias = bs_ref[:, gi:gi + 1]
        for c in range(tm // SGU_LEN):
            rows = slice(c * SGU_LEN, (c + 1) * SGU_LEN)
            cols = slice(gi * gd, (gi + 1) * gd)
            vm = _dot(wm, v[rows, cols]) + bias
            o_ref[rows, cols] = (u[rows, cols] * vm).astype(o_ref.dtype)


def _sgu(x, g, w, ln_g, ln_b, w_s, b_s_t, name):
    rows, d = x.shape
    wd = w.shape[1] // 2
    tm = SGU_TM
    const2 = lambda i: (0, 0)
    return pl.pallas_call(
        _sgu_kernel,
        out_shape=jax.ShapeDtypeStruct((rows, wd), BF16),
        grid=(rows // tm,),
        in_specs=[
            pl.BlockSpec((tm, d), lambda i: (i, 0)),
            pl.BlockSpec((1, d), const2),
            pl.BlockSpec((d, 2 * wd), const2),
            pl.BlockSpec((1, wd), const2),
            pl.BlockSpec((1, wd), const2),
            pl.BlockSpec((SGU_GROUPS, SGU_LEN, SGU_LEN), lambda i: (0, 0, 0)),
            pl.BlockSpec((SGU_LEN, SGU_GROUPS), const2),
        ],
        out_specs=pl.BlockSpec((tm, wd), lambda i: (i, 0)),
        compiler_params=_params("parallel"),
        name=name,
    )(x, g, w, ln_g, ln_b, w_s, b_s_t)


def _conv_kernel(x_ref, g_ref, w_ref, cw_ref, o_ref, halo_ref, *, tiles_per_seq):
    tm = x_ref.shape[0]
    wd = o_ref.shape[1]
    h = _rms(x_ref[...], g_ref[...]).astype(BF16)
    p = _dot(h, w_ref[...])
    cb = p[:, :wd]
    y = p[:, wd:2 * wd] * p[:, 2 * wd:]
    seq_start = pl.program_id(0) % tiles_per_seq == 0
    halo = jnp.where(seq_start, 0.0, halo_ref[...])
    prev1 = halo[CONV_HALO - 1:CONV_HALO, :]
    prev2 = halo[CONV_HALO - 2:CONV_HALO - 1, :]
    row = lax.broadcasted_iota(jnp.int32, (tm, wd), 0)
    y1 = jnp.where(row == 0, prev1, pltpu.roll(y, 1, axis=0))
    y2 = jnp.where(row == 0, prev2, jnp.where(row == 1, prev1, pltpu.roll(y, 2, axis=0)))
    cw = cw_ref[...]
    conv = cw[0:1, :] * y2 + cw[1:2, :] * y1 + cw[2:3, :] * y
    o_ref[...] = (cb * conv).astype(o_ref.dtype)
    halo_ref[...] = y[tm - CONV_HALO:, :]


def _gated_conv(x, g, w, conv_w, seq, name):
    rows, d = x.shape
    wd = w.shape[1] // 3
    tm = CONV_TM
    const2 = lambda i: (0, 0)
    return pl.pallas_call(
        functools.partial(_conv_kernel, tiles_per_seq=seq // tm),
        out_shape=jax.ShapeDtypeStruct((rows, wd), BF16),
        grid=(rows // tm,),
        in_specs=[
            pl.BlockSpec((tm, d), lambda i: (i, 0)),
            pl.BlockSpec((1, d), const2),
            pl.BlockSpec((d, 3 * wd), const2),
            pl.BlockSpec((CONV_WIDTH, wd), const2),
        ],
        out_specs=pl.BlockSpec((tm, wd), lambda i: (i, 0)),
        scratch_shapes=[pltpu.VMEM((CONV_HALO, wd), F32)],
        compiler_params=_params("arbitrary"),
        name=name,
    )(x, g, w, conv_w)


def _merge_kernel(x_ref, g_ref, wg_ref, ya_ref, yb_ref, yc_ref, wb_ref, wo_ref, o_ref):
    d = x_ref.shape[1]
    x = x_ref[...]
    h = _rms(x, g_ref[...]).astype(BF16)
    merged = None
    for n, y_ref in enumerate((ya_ref, yb_ref, yc_ref)):
        gate = jax.nn.sigmoid(_dot(h, wg_ref[:, n * d:(n + 1) * d]))
        term = gate * _dot(y_ref[...], wb_ref[n])
        merged = term if merged is None else merged + term
    o_ref[...] = x + _dot(merged.astype(BF16), wo_ref[...])


def _merge(x, g, w_gates, ya, yb, yc, w_branch, w_out, name):
    rows, d = x.shape
    wd = ya.shape[1]
    tm = MERGE_TM
    const2 = lambda i: (0, 0)
    row_d = pl.BlockSpec((tm, d), lambda i: (i, 0))
    row_w = pl.BlockSpec((tm, wd), lambda i: (i, 0))
    return pl.pallas_call(
        _merge_kernel,
        out_shape=jax.ShapeDtypeStruct((rows, d), F32),
        grid=(rows // tm,),
        in_specs=[
            row_d,
            pl.BlockSpec((1, d), const2),
            pl.BlockSpec((d, 3 * d), const2),
            row_w, row_w, row_w,
            pl.BlockSpec((3, wd, d), lambda i: (0, 0, 0)),
            pl.BlockSpec((d, d), const2),
        ],
        out_specs=row_d,
        compiler_params=_params("parallel"),
        name=name,
    )(x, g, w_gates, ya, yb, yc, w_branch, w_out)


def _xattn_kernel(x_ref, g_ref, wq_ref, k_ref, v_ref, wo_ref, o_ref):
    d = x_ref.shape[1]
    dh = d // XA_HEADS
    x = x_ref[...]
    h = _rms(x, g_ref[...]).astype(BF16)
    q = (_dot(h, wq_ref[...]) * (dh ** -0.5)).astype(BF16)
    heads = []
    for hd in range(XA_HEADS):
        cols = slice(hd * dh, (hd + 1) * dh)
        s = _dot_nt(q[:, cols], k_ref[:, cols])
        p = jnp.exp(s - jnp.max(s, axis=-1, keepdims=True))
        p = p / jnp.sum(p, axis=-1, keepdims=True)
        heads.append(_dot(p.astype(BF16), v_ref[:, cols]).astype(BF16))
    o_ref[...] = x + _dot(jnp.concatenate(heads, axis=1), wo_ref[...])


def _xattn(x, g, wq, kv, wo, name):
    b, s, d = x.shape
    mem = kv.shape[1]
    tm = XA_TM
    const2 = lambda bi, i: (0, 0)
    row_d = pl.BlockSpec((None, tm, d), lambda bi, i: (bi, i, 0))
    return pl.pallas_call(
        _xattn_kernel,
        out_shape=jax.ShapeDtypeStruct((b, s, d), F32),
        grid=(b, s // tm),
        in_specs=[
            row_d,
            pl.BlockSpec((1, d), const2),
            pl.BlockSpec((d, d), const2),
            pl.BlockSpec((None, mem, d), lambda bi, i: (bi, 0, 0)),
            pl.BlockSpec((None, mem, d), lambda bi, i: (bi, 0, 1)),
            pl.BlockSpec((d, d), const2),
        ],
        out_specs=row_d,
        compiler_params=_params("parallel", "parallel"),
        name=name,
    )(x, g, wq, kv, kv, wo)


def _ffn_kernel(x_ref, g_ref, wg_ref, wu_ref, wd_ref, fg_ref, o_ref, h_ref, acc_ref, *, final_norm):
    j = pl.program_id(1)

    @pl.when(j == 0)
    def _():
        h_ref[...] = _rms(x_ref[...], g_ref[...]).astype(BF16)
        acc_ref[...] = jnp.zeros_like(acc_ref)

    h = h_ref[...]
    a = _dot(h, wg_ref[...])
    t = (a * jax.nn.sigmoid(a)) * _dot(h, wu_ref[...])
    acc_ref[...] += _dot(t.astype(BF16), wd_ref[...])

    @pl.when(j == pl.num_programs(1) - 1)
    def _():
        y = x_ref[...] + acc_ref[...]
        if final_norm:
            y = _rms(y, fg_ref[...])
        o_ref[...] = y


def _ffn(x, g, w_gate, w_up, w_down, final_g, final_norm, name):
    rows, d = x.shape
    hid = w_gate.shape[1]
    tm, th = FFN_TM, FFN_TH
    const2 = lambda i, j: (0, 0)
    row_d = pl.BlockSpec((tm, d), lambda i, j: (i, 0))
    return pl.pallas_call(
        functools.partial(_ffn_kernel, final_norm=final_norm),
        out_shape=jax.ShapeDtypeStruct((rows, d), F32),
        grid=(rows // tm, hid // th),
        in_specs=[
            row_d,
            pl.BlockSpec((1, d), const2),
            pl.BlockSpec((d, th), lambda i, j: (0, j)),
            pl.BlockSpec((d, th), lambda i, j: (0, j)),
            pl.BlockSpec((th, d), lambda i, j: (j, 0)),
            pl.BlockSpec((1, d), const2),
        ],
        out_specs=row_d,
        scratch_shapes=[pltpu.VMEM((tm, d), BF16), pltpu.VMEM((tm, d), F32)],
        compiler_params=_params("parallel", "arbitrary"),
        name=name,
    )(x, g, w_gate, w_up, w_down, final_g)


def kernel(x, mem, norm_mix_g, w_in, sgu_ln_g, sgu_ln_b, w_spatial, b_spatial, conv_w, w_branch, w_out, norm_xa_g, mem_norm_g, w_q_xa, w_k_xa, w_v_xa, w_o_xa, norm_ffn_g, w_gate_ffn, w_up_ffn, w_down_ffn, final_g):
    b, s, d = x.shape
    depth = w_in.shape[0]
    wd = w_branch.shape[2]
    n_mem = mem.shape[1]
    assert d % LANES == 0 and s % max(ATT_TQ, ATT_KG, SGU_TM, CONV_TM, MERGE_TM, XA_TM) == 0
    assert ATT_KG % ATT_KB == 0 and ATT_KB % LANES == 0
    assert (b * s) % max(PROJ_TM, FFN_TM) == 0 and wd == SB_HEADS * SB_HEAD_DIM
    assert w_gate_ffn.shape[2] % FFN_TH == 0 and CONV_TM % SGU_LEN == 0

    c_qkv, c_sgu, c_conv = 3 * wd, 5 * wd, 8 * wd
    row = lambda v: v.reshape(1, -1)
    xf = x.reshape(b * s, d)
    memf = mem.reshape(b * n_mem, d)
    for l in range(depth):
        w_in_l = w_in[l].astype(BF16)
        g_mix = row(norm_mix_g[l])
        qkv = _norm_matmul(xf, g_mix, w_in_l[:, :c_qkv], f"qkv_{l}")
        ya = _sb_attention(qkv.reshape(b, s, c_qkv), f"sb_attn_{l}").reshape(b * s, wd)
        yb = _sgu(xf, g_mix, w_in_l[:, c_qkv:c_sgu], row(sgu_ln_g[l]), row(sgu_ln_b[l]),
                  w_spatial[l], b_spatial[l].T, f"sgu_{l}")
        yc = _gated_conv(xf, g_mix, w_in_l[:, c_sgu:c_conv], conv_w[l], s, f"conv_{l}")
        xf = _merge(xf, g_mix, w_in_l[:, c_conv:], ya, yb, yc, w_branch[l].astype(BF16),
                    w_out[l].astype(BF16), f"merge_{l}")
        w_kv = jnp.concatenate([w_k_xa[l], w_v_xa[l]], axis=1).astype(BF16)
        kv = _norm_matmul(memf, row(mem_norm_g[l]), w_kv, f"mem_kv_{l}").reshape(b, n_mem, 2 * d)
        xf = _xattn(xf.reshape(b, s, d), row(norm_xa_g[l]), w_q_xa[l].astype(BF16), kv,
                    w_o_xa[l].astype(BF16), f"xattn_{l}").reshape(b * s, d)
        xf = _ffn(xf, row(norm_ffn_g[l]), w_gate_ffn[l].astype(BF16), w_up_ffn[l].astype(BF16),
                  w_down_ffn[l].astype(BF16), row(final_g), l == depth - 1, f"ffn_{l}")
    return xf.reshape(b, s, d)
```

```python
import functools

import jax
import jax.numpy as jnp
from jax import lax
from jax.experimental import pallas as pl
from jax.experimental.pallas import tpu as pltpu

F32 = jnp.float32
BF16 = jnp.bfloat16

LANES = 128
V7X_VMEM_LIMIT_BYTES = 56 * 1024 * 1024

SB_HEADS = 8
SB_HEAD_DIM = 64
HEADS_PER_LANE_TILE = LANES // SB_HEAD_DIM
SGU_LEN = 128
SGU_GROUPS = 4
CHUNK = 64
CONV_WIDTH = 3
XA_HEADS = 4
RMS_EPS = 1e-6
LOG2E = 1.4426950408889634
MASKED_LOG_WEIGHT = -1e30
LN_EPS = 1e-5

PROJ_TM = 1024
PROJ_TN = 512
ATT_TQ = 256
ATT_KG = 512
ATT_KB = 256
SGU_TM = 512
CONV_TM = 512
MERGE_TM = 512
XA_TM = 512
FFN_TM = 1024
FFN_TH = 256
CONV_HALO = 8


def _params(*semantics):
    return pltpu.CompilerParams(dimension_semantics=semantics,
                                vmem_limit_bytes=V7X_VMEM_LIMIT_BYTES)


def _rms(x, g):
    return x * lax.rsqrt(jnp.mean(x * x, axis=-1, keepdims=True) + RMS_EPS) * g


def _dot(a, b):
    return jnp.dot(a, b, preferred_element_type=F32)


def _dot_nt(a, b):
    return lax.dot_general(a, b, (((1,), (1,)), ((), ())), preferred_element_type=F32)


def _norm_matmul_kernel(x_ref, g_ref, w_ref, o_ref, h_ref):
    @pl.when(pl.program_id(1) == 0)
    def _():
        h_ref[...] = _rms(x_ref[...], g_ref[...]).astype(BF16)

    o_ref[...] = _dot(h_ref[...], w_ref[...]).astype(o_ref.dtype)


def _norm_matmul(x, g, w, name):
    rows, d = x.shape
    cols = w.shape[1]
    tm = min(PROJ_TM, rows)
    return pl.pallas_call(
        _norm_matmul_kernel,
        out_shape=jax.ShapeDtypeStruct((rows, cols), BF16),
        grid=(rows // tm, cols // PROJ_TN),
        in_specs=[
            pl.BlockSpec((tm, d), lambda i, j: (i, 0)),
            pl.BlockSpec((1, d), lambda i, j: (0, 0)),
            pl.BlockSpec((d, PROJ_TN), lambda i, j: (0, j)),
        ],
        out_specs=pl.BlockSpec((tm, PROJ_TN), lambda i, j: (i, j)),
        scratch_shapes=[pltpu.VMEM((tm, d), BF16)],
        compiler_params=_params("parallel", "arbitrary"),
        name=name,
    )(x, g, w)


def _sb_attn_kernel(q_ref, k_ref, v_ref, tri_ref, o_ref, acc_ref, carry_ref, ls_ref, l16_ref, tot_ref):
    tq, kg, kb = ATT_TQ, ATT_KG, ATT_KB
    n_blocks = kg // kb
    i = pl.program_id(2)
    lane = lax.broadcasted_iota(jnp.int32, (tq, LANES), 1)
    q = q_ref[...] * jnp.asarray(SB_HEAD_DIM ** -0.5, BF16)
    zero = jnp.zeros_like(q)
    q2 = jnp.concatenate([jnp.where(lane < SB_HEAD_DIM, q, zero),
                          jnp.where(lane >= SB_HEAD_DIM, q, zero)], axis=0)

    def logits(g):
        ks = k_ref[pl.ds(pl.multiple_of(g * kg, kg), kg), :]
        return _dot_nt(q2, ks)

    def scores(z, slot, mask=None):
        sp = jnp.log(1.0 + jnp.exp2(jnp.abs(z) * -LOG2E))
        log_sig = jnp.minimum(z, 0.0) - sp
        log_1m = log_sig - z
        if mask is not None:
            log_1m = jnp.where(mask, log_1m, 0.0)
            log_sig = jnp.where(mask, log_sig, MASKED_LOG_WEIGHT)
        ls_ref[slot] = log_sig
        l16_ref[slot] = log_1m.astype(BF16)
        for n in range(n_blocks):
            tot_ref[slot, n] = jnp.sum(log_1m[:, n * kb:(n + 1) * kb], axis=-1, keepdims=True)

    def weights(g, slot):
        vs = v_ref[pl.ds(pl.multiple_of(g * kg, kg), kg), :]
        carry = carry_ref[...]
        log_a = [None] * n_blocks
        for n in reversed(range(n_blocks)):
            cols = slice(n * kb, (n + 1) * kb)
            log_a[n] = ls_ref[slot, :, cols] + _dot(l16_ref[slot, :, cols], tri_ref[...]) + carry
            carry = carry + tot_ref[slot, n]
        carry_ref[...] = carry
        a = jnp.exp(jnp.concatenate(log_a, axis=1))
        acc_ref[...] += _dot(a.astype(BF16), vs)

    acc_ref[...] = jnp.zeros_like(acc_ref)
    carry_ref[...] = jnp.zeros_like(carry_ref)
    n_groups = ((i + 1) * tq + kg - 1) // kg
    top = n_groups - 1
    row = lax.broadcasted_iota(jnp.int32, (2 * tq, kg), 0)
    col = lax.broadcasted_iota(jnp.int32, (2 * tq, kg), 1)
    t_pos = i * tq + jnp.where(row >= tq, row - tq, row)
    scores(logits(top), 0, top * kg + col < t_pos)

    def two_groups(m, c):
        g = top - 2 * m
        z = logits(g - 1)
        weights(g, 0)
        scores(z, 1)
        z = logits(g - 2)
        weights(g - 1, 1)
        scores(z, 0)
        return c

    lax.fori_loop(0, top // 2, two_groups, 0)

    @pl.when(top % 2 == 1)
    def _():
        z = logits(0)
        weights(1, 0)
        scores(z, 1)
        weights(0, 1)

    @pl.when(top % 2 == 0)
    def _():
        weights(0, 0)

    acc = acc_ref[...]
    o_ref[...] = jnp.where(lane < SB_HEAD_DIM, acc[:tq], acc[tq:]).astype(o_ref.dtype)


def _tri_matrix():
    j = lax.broadcasted_iota(jnp.int32, (ATT_KB, ATT_KB), 0)
    s = lax.broadcasted_iota(jnp.int32, (ATT_KB, ATT_KB), 1)
    return (j > s).astype(BF16)


def _sb_attention(qkv, name):
    b, s, _ = qkv.shape
    tq = ATT_TQ
    width = SB_HEADS * SB_HEAD_DIM
    pairs = width // LANES
    return pl.pallas_call(
        _sb_attn_kernel,
        out_shape=jax.ShapeDtypeStruct((b, s, width), BF16),
        grid=(b, pairs, s // tq),
        in_specs=[
            pl.BlockSpec((None, tq, LANES), lambda bi, p, i: (bi, i, p)),
            pl.BlockSpec((None, s, LANES), lambda bi, p, i: (bi, 0, pairs + p)),
            pl.BlockSpec((None, s, LANES), lambda bi, p, i: (bi, 0, 2 * pairs + p)),
            pl.BlockSpec((ATT_KB, ATT_KB), lambda bi, p, i: (0, 0)),
        ],
        out_specs=pl.BlockSpec((None, tq, LANES), lambda bi, p, i: (bi, i, p)),
        scratch_shapes=[
            pltpu.VMEM((2 * tq, LANES), F32),
            pltpu.VMEM((2 * tq, 1), F32),
            pltpu.VMEM((2, 2 * tq, ATT_KG), F32),
            pltpu.VMEM((2, 2 * tq, ATT_KG), BF16),
            pltpu.VMEM((2, ATT_KG // ATT_KB, 2 * tq, 1), F32),
        ],
        compiler_params=_params("parallel", "parallel", "arbitrary"),
        name=name,
    )(qkv, qkv, qkv, _tri_matrix())


def _sgu_kernel(x_ref, g_ref, w_ref, lng_ref, lnb_ref, ws_ref, bs_ref, o_ref):
    tm = x_ref.shape[0]
    wd = o_ref.shape[1]
    gd = wd // SGU_GROUPS
    h = _rms(x_ref[...], g_ref[...]).astype(BF16)
    z = _dot(h, w_ref[...])
    z = 0.5 * z * (1.0 + lax.erf(z * (2.0 ** -0.5)))
    u = z[:, :wd]
    v = z[:, wd:]
    mu = jnp.mean(v, axis=-1, keepdims=True)
    vc = v - mu
    v = vc * lax.rsqrt(jnp.mean(vc * vc, axis=-1, keepdims=True) + LN_EPS)
    v = (v * lng_ref[...] + lnb_ref[...]).astype(BF16)
    pos_t = lax.broadcasted_iota(jnp.int32, (SGU_LEN, SGU_LEN), 0)
    pos_s = lax.broadcasted_iota(jnp.int32, (SGU_LEN, SGU_LEN), 1)
    allowed = (pos_t // CHUNK) >= (pos_s // CHUNK)
    for gi in range(SGU_GROUPS):
        wm = jnp.where(allowed, ws_ref[gi], 0.0).astype(BF16)
        bias = bs_ref[:, gi:gi + 1]
        for c in range(tm // SGU_LEN):
            rows = slice(c * SGU_LEN, (c + 1) * SGU_LEN)
            cols = slice(gi * gd, (gi + 1) * gd)
            vm = _dot(wm, v[rows, cols]) + bias
            o_ref[rows, cols] = (u[rows, cols] * vm).astype(o_ref.dtype)


def _sgu(x, g, w, ln_g, ln_b, w_s, b_s_t, name):
    rows, d = x.shape
    wd = w.shape[1] // 2
    tm = SGU_TM
    const2 = lambda i: (0, 0)
    return pl.pallas_call(
        _sgu_kernel,
        out_shape=jax.ShapeDtypeStruct((rows, wd), BF16),
        grid=(rows // tm,),
        in_specs=[
            pl.BlockSpec((tm, d), lambda i: (i, 0)),
            pl.BlockSpec((1, d), const2),
            pl.BlockSpec((d, 2 * wd), const2),
            pl.BlockSpec((1, wd), const2),
            pl.BlockSpec((1, wd), const2),
            pl.BlockSpec((SGU_GROUPS, SGU_LEN, SGU_LEN), lambda i: (0, 0, 0)),
            pl.BlockSpec((SGU_LEN, SGU_GROUPS), const2),
        ],
        out_specs=pl.BlockSpec((tm, wd), lambda i: (i, 0)),
        compiler_params=_params("parallel"),
        name=name,
    )(x, g, w, ln_g, ln_b, w_s, b_s_t)


def _conv_kernel(x_ref, g_ref, w_ref, cw_ref, o_ref, halo_ref, *, tiles_per_seq):
    tm = x_ref.shape[0]
    wd = o_ref.shape[1]
    h = _rms(x_ref[...], g_ref[...]).astype(BF16)
    p = _dot(h, w_ref[...])
    cb = p[:, :wd]
    y = p[:, wd:2 * wd] * p[:, 2 * wd:]
    seq_start = pl.program_id(0) % tiles_per_seq == 0
    halo = jnp.where(seq_start, 0.0, halo_ref[...])
    prev1 = halo[CONV_HALO - 1:CONV_HALO, :]
    prev2 = halo[CONV_HALO - 2:CONV_HALO - 1, :]
    row = lax.broadcasted_iota(jnp.int32, (tm, wd), 0)
    y1 = jnp.where(row == 0, prev1, pltpu.roll(y, 1, axis=0))
    y2 = jnp.where(row == 0, prev2, jnp.where(row == 1, prev1, pltpu.roll(y, 2, axis=0)))
    cw = cw_ref[...]
    conv = cw[0:1, :] * y2 + cw[1:2, :] * y1 + cw[2:3, :] * y
    o_ref[...] = (cb * conv).astype(o_ref.dtype)
    halo_ref[...] = y[tm - CONV_HALO:, :]


def _gated_conv(x, g, w, conv_w, seq, name):
    rows, d = x.shape
    wd = w.shape[1] // 3
    tm = CONV_TM
    const2 = lambda i: (0, 0)
    return pl.pallas_call(
        functools.partial(_conv_kernel, tiles_per_seq=seq // tm),
        out_shape=jax.ShapeDtypeStruct((rows, wd), BF16),
        grid=(rows // tm,),
        in_specs=[
            pl.BlockSpec((tm, d), lambda i: (i, 0)),
            pl.BlockSpec((1, d), const2),
            pl.BlockSpec((d, 3 * wd), const2),
            pl.BlockSpec((CONV_WIDTH, wd), const2),
        ],
        out_specs=pl.BlockSpec((tm, wd), lambda i: (i, 0)),
        scratch_shapes=[pltpu.VMEM((CONV_HALO, wd), F32)],
        compiler_params=_params("arbitrary"),
        name=name,
    )(x, g, w, conv_w)


def _merge_kernel(x_ref, g_ref, wg_ref, ya_ref, yb_ref, yc_ref, wb_ref, wo_ref, o_ref):
    d = x_ref.shape[1]
    x = x_ref[...]
    h = _rms(x, g_ref[...]).astype(BF16)
    merged = None
    for n, y_ref in enumerate((ya_ref, yb_ref, yc_ref)):
        gate = jax.nn.sigmoid(_dot(h, wg_ref[:, n * d:(n + 1) * d]))
        term = gate * _dot(y_ref[...], wb_ref[n])
        merged = term if merged is None else merged + term
    o_ref[...] = x + _dot(merged.astype(BF16), wo_ref[...])


def _merge(x, g, w_gates, ya, yb, yc, w_branch, w_out, name):
    rows, d = x.shape
    wd = ya.shape[1]
    tm = MERGE_TM
    const2 = lambda i: (0, 0)
    row_d = pl.BlockSpec((tm, d), lambda i: (i, 0))
    row_w = pl.BlockSpec((tm, wd), lambda i: (i, 0))
    return pl.pallas_call(
        _merge_kernel,
        out_shape=jax.ShapeDtypeStruct((rows, d), F32),
        grid=(rows // tm,),
        in_specs=[
            row_d,
            pl.BlockSpec((1, d), const2),
            pl.BlockSpec((d, 3 * d), const2),
            row_w, row_w, row_w,
            pl.BlockSpec((3, wd, d), lambda i: (0, 0, 0)),
            pl.BlockSpec((d, d), const2),
        ],
        out_specs=row_d,
        compiler_params=_params("parallel"),
        name=name,
    )(x, g, w_gates, ya, yb, yc, w_branch, w_out)


def _xattn_kernel(x_ref, g_ref, wq_ref, k_ref, v_ref, wo_ref, o_ref):
    d = x_ref.shape[1]
    dh = d // XA_HEADS
    x = x_ref[...]
    h = _rms(x, g_ref[...]).astype(BF16)
    q = (_dot(h, wq_ref[...]) * (dh ** -0.5)).astype(BF16)
    heads = []
    for hd in range(XA_HEADS):
        cols = slice(hd * dh, (hd + 1) * dh)
        s = _dot_nt(q[:, cols], k_ref[:, cols])
        p = jnp.exp(s - jnp.max(s, axis=-1, keepdims=True))
        p = p / jnp.sum(p, axis=-1, keepdims=True)
        heads.append(_dot(p.astype(BF16), v_ref[:, cols]).astype(BF16))
    o_ref[...] = x + _dot(jnp.concatenate(heads, axis=1), wo_ref[...])


def _xattn(x, g, wq, kv, wo, name):
    b, s, d = x.shape
    mem = kv.shape[1]
    tm = XA_TM
    const2 = lambda bi, i: (0, 0)
    row_d = pl.BlockSpec((None, tm, d), lambda bi, i: (bi, i, 0))
    return pl.pallas_call(
        _xattn_kernel,
        out_shape=jax.ShapeDtypeStruct((b, s, d), F32),
        grid=(b, s // tm),
        in_specs=[
            row_d,
            pl.BlockSpec((1, d), const2),
            pl.BlockSpec((d, d), const2),
            pl.BlockSpec((None, mem, d), lambda bi, i: (bi, 0, 0)),
            pl.BlockSpec((None, mem, d), lambda bi, i: (bi, 0, 1)),
            pl.BlockSpec((d, d), const2),
        ],
        out_specs=row_d,
        compiler_params=_params("parallel", "parallel"),
        name=name,
    )(x, g, wq, kv, kv, wo)


def _ffn_kernel(x_ref, g_ref, wg_ref, wu_ref, wd_ref, fg_ref, o_ref, h_ref, acc_ref, *, final_norm):
    j = pl.program_id(1)

    @pl.when(j == 0)
    def _():
        h_ref[...] = _rms(x_ref[...], g_ref[...]).astype(BF16)
        acc_ref[...] = jnp.zeros_like(acc_ref)

    h = h_ref[...]
    a = _dot(h, wg_ref[...])
    t = (a * jax.nn.sigmoid(a)) * _dot(h, wu_ref[...])
    acc_ref[...] += _dot(t.astype(BF16), wd_ref[...])

    @pl.when(j == pl.num_programs(1) - 1)
    def _():
        y = x_ref[...] + acc_ref[...]
        if final_norm:
            y = _rms(y, fg_ref[...])
        o_ref[...] = y


def _ffn(x, g, w_gate, w_up, w_down, final_g, final_norm, name):
    rows, d = x.shape
    hid = w_gate.shape[1]
    tm, th = FFN_TM, FFN_TH
    const2 = lambda i, j: (0, 0)
    row_d = pl.BlockSpec((tm, d), lambda i, j: (i, 0))
    return pl.pallas_call(
        functools.partial(_ffn_kernel, final_norm=final_norm),
        out_shape=jax.ShapeDtypeStruct((rows, d), F32),
        grid=(rows // tm, hid // th),
        in_specs=[
            row_d,
            pl.BlockSpec((1, d), const2),
            pl.BlockSpec((d, th), lambda i, j: (0, j)),
            pl.BlockSpec((d, th), lambda i, j: (0, j)),
            pl.BlockSpec((th, d), lambda i, j: (j, 0)),
            pl.BlockSpec((1, d), const2),
        ],
        out_specs=row_d,
        scratch_shapes=[pltpu.VMEM((tm, d), BF16), pltpu.VMEM((tm, d), F32)],
        compiler_params=_params("parallel", "arbitrary"),
        name=name,
    )(x, g, w_gate, w_up, w_down, final_g)


def kernel(x, mem, norm_mix_g, w_in, sgu_ln_g, sgu_ln_b, w_spatial, b_spatial, conv_w, w_branch, w_out, norm_xa_g, mem_norm_g, w_q_xa, w_k_xa, w_v_xa, w_o_xa, norm_ffn_g, w_gate_ffn, w_up_ffn, w_down_ffn, final_g):
    b, s, d = x.shape
    depth = w_in.shape[0]
    wd = w_branch.shape[2]
    n_mem = mem.shape[1]
    assert d % LANES == 0 and s % max(ATT_TQ, ATT_KG, SGU_TM, CONV_TM, MERGE_TM, XA_TM) == 0
    assert ATT_KG % ATT_KB == 0 and ATT_KB % LANES == 0
    assert (b * s) % max(PROJ_TM, FFN_TM) == 0 and wd == SB_HEADS * SB_HEAD_DIM
    assert w_gate_ffn.shape[2] % FFN_TH == 0 and CONV_TM % SGU_LEN == 0

    c_qkv, c_sgu, c_conv = 3 * wd, 5 * wd, 8 * wd
    row = lambda v: v.reshape(1, -1)
    xf = x.reshape(b * s, d)
    memf = mem.reshape(b * n_mem, d)
    for l in range(depth):
        w_in_l = w_in[l].astype(BF16)
        g_mix = row(norm_mix_g[l])
        qkv = _norm_matmul(xf, g_mix, w_in_l[:, :c_qkv], f"qkv_{l}")
        ya = _sb_attention(qkv.reshape(b, s, c_qkv), f"sb_attn_{l}").reshape(b * s, wd)
        yb = _sgu(xf, g_mix, w_in_l[:, c_qkv:c_sgu], row(sgu_ln_g[l]), row(sgu_ln_b[l]),
                  w_spatial[l], b_spatial[l].T, f"sgu_{l}")
        yc = _gated_conv(xf, g_mix, w_in_l[:, c_sgu:c_conv], conv_w[l], s, f"conv_{l}")
        xf = _merge(xf, g_mix, w_in_l[:, c_conv:], ya, yb, yc, w_branch[l].astype(BF16),
                    w_out[l].astype(BF16), f"merge_{l}")
        w_kv = jnp.concatenate([w_k_xa[l], w_v_xa[l]], axis=1).astype(BF16)
        kv = _norm_matmul(memf, row(mem_norm_g[l]), w_kv, f"mem_kv_{l}").reshape(b, n_mem, 2 * d)
        xf = _xattn(xf.reshape(b, s, d), row(norm_xa_g[l]), w_q_xa[l].astype(BF16), kv,
                    w_o_xa[l].astype(BF16), f"xattn_{l}").reshape(b * s, d)
        xf = _ffn(xf, row(norm_ffn_g[l]), w_gate_ffn[l].astype(BF16), w_up_ffn[l].astype(BF16),
                  w_down_ffn[l].astype(BF16), row(final_g), l == depth - 1, f"ffn_{l}")
    return xf.reshape(b, s, d)
```

```python
import functools

import jax
import jax.numpy as jnp
from jax import lax
from jax.experimental import pallas as pl
from jax.experimental.pallas import tpu as pltpu

F32 = jnp.float32
BF16 = jnp.bfloat16

LANES = 128
V7X_VMEM_LIMIT_BYTES = 56 * 1024 * 1024

SB_HEADS = 8
SB_HEAD_DIM = 64
SGU_LEN = 128
SGU_GROUPS = 4
CHUNK = 64
CONV_WIDTH = 3
N_BRANCHES = 3
XA_HEADS = 4
RMS_EPS = 1e-6
LN_EPS = 1e-5
LOG2E = 1.4426950408889634
MASKED_LOG_WEIGHT = -1e30
DEAD_LOG_WEIGHT = -120.0

PROJ_TM = 1024
PROJ_TN = 512
MIX_TM = 512
ATT_TQ = 256
ATT_KG = 256
MERGE_TM = 512
XA_TM = 512
FFN_TM = 1024
FFN_TH = 256
CONV_HALO = 8


def _params(*semantics):
    return pltpu.CompilerParams(dimension_semantics=semantics,
                                vmem_limit_bytes=V7X_VMEM_LIMIT_BYTES)


def _layer_spec(shape, layer):
    zeros = (0,) * len(shape)
    return pl.BlockSpec((None, *shape), lambda *_: (layer, *zeros))


def _rms(x, g):
    return x * lax.rsqrt(jnp.mean(x * x, axis=-1, keepdims=True) + RMS_EPS) * g


def _dot(a, b):
    return jnp.dot(a, b, preferred_element_type=F32)


def _dot_nt(a, b):
    return lax.dot_general(a, b, (((1,), (1,)), ((), ())), preferred_element_type=F32)


def _norm_matmul_kernel(x_ref, g_ref, w_ref, o_ref, h_ref):
    @pl.when(pl.program_id(1) == 0)
    def _():
        h_ref[...] = _rms(x_ref[...], g_ref[...]).astype(BF16)

    o_ref[...] = _dot(h_ref[...], w_ref[...]).astype(o_ref.dtype)


def _norm_matmul(x, g, w, layer, name):
    rows, d = x.shape
    cols = w.shape[2]
    tm = min(PROJ_TM, rows)
    return pl.pallas_call(
        _norm_matmul_kernel,
        out_shape=jax.ShapeDtypeStruct((rows, cols), BF16),
        grid=(rows // tm, cols // PROJ_TN),
        in_specs=[
            pl.BlockSpec((tm, d), lambda i, j: (i, 0)),
            _layer_spec((1, d), layer),
            pl.BlockSpec((None, d, PROJ_TN), lambda i, j: (layer, 0, j)),
        ],
        out_specs=pl.BlockSpec((tm, PROJ_TN), lambda i, j: (i, j)),
        scratch_shapes=[pltpu.VMEM((tm, d), BF16)],
        compiler_params=_params("parallel", "arbitrary"),
        name=name,
    )(x, g, w)


def _spatial_gating(z, lng_ref, lnb_ref, ws_ref, bs_ref, o_ref):
    tm = z.shape[0]
    wd = o_ref.shape[1]
    gd = wd // SGU_GROUPS
    z = 0.5 * z * (1.0 + lax.erf(z * (2.0 ** -0.5)))
    u = z[:, :wd]
    v = z[:, wd:]
    mu = jnp.mean(v, axis=-1, keepdims=True)
    vc = v - mu
    v = vc * lax.rsqrt(jnp.mean(vc * vc, axis=-1, keepdims=True) + LN_EPS)
    v = (v * lng_ref[...] + lnb_ref[...]).astype(BF16)
    pos_t = lax.broadcasted_iota(jnp.int32, (SGU_LEN, SGU_LEN), 0)
    pos_s = lax.broadcasted_iota(jnp.int32, (SGU_LEN, SGU_LEN), 1)
    allowed = (pos_t // CHUNK) >= (pos_s // CHUNK)
    for gi in range(SGU_GROUPS):
        wm = jnp.where(allowed, ws_ref[gi], 0.0).astype(BF16)
        bias = bs_ref[:, gi:gi + 1]
        for c in range(tm // SGU_LEN):
            rows = slice(c * SGU_LEN, (c + 1) * SGU_LEN)
            cols = slice(gi * gd, (gi + 1) * gd)
            vm = _dot(wm, v[rows, cols]) + bias
            o_ref[rows, cols] = (u[rows, cols] * vm).astype(o_ref.dtype)


def _gated_conv(p, cw_ref, o_ref, halo_ref, seq_start):
    tm = p.shape[0]
    wd = o_ref.shape[1]
    cb = p[:, :wd]
    y = p[:, wd:2 * wd] * p[:, 2 * wd:]
    halo = jnp.where(seq_start, 0.0, halo_ref[...])
    prev1 = halo[CONV_HALO - 1:CONV_HALO, :]
    prev2 = halo[CONV_HALO - 2:CONV_HALO - 1, :]
    row = lax.broadcasted_iota(jnp.int32, (tm, wd), 0)
    y1 = jnp.where(row == 0, prev1, pltpu.roll(y, 1, axis=0))
    y2 = jnp.where(row == 0, prev2, jnp.where(row == 1, prev1, pltpu.roll(y, 2, axis=0)))
    cw = cw_ref[...]
    conv = cw[0:1, :] * y2 + cw[1:2, :] * y1 + cw[2:3, :] * y
    o_ref[...] = (cb * conv).astype(o_ref.dtype)
    halo_ref[...] = y[tm - CONV_HALO:, :]


def _mix_in_kernel(x_ref, g_ref, w_ref, lng_ref, lnb_ref, ws_ref, bs_ref, cw_ref,
                   qkv_ref, yb_ref, yc_ref, halo_ref, *, tiles_per_seq):
    wd = yb_ref.shape[1]
    c_qkv, c_sgu = 3 * wd, 5 * wd
    h = _rms(x_ref[...], g_ref[...]).astype(BF16)
    qkv_ref[...] = _dot(h, w_ref[:, :c_qkv]).astype(qkv_ref.dtype)
    _spatial_gating(_dot(h, w_ref[:, c_qkv:c_sgu]), lng_ref, lnb_ref, ws_ref, bs_ref, yb_ref)
    _gated_conv(_dot(h, w_ref[:, c_sgu:]), cw_ref, yc_ref, halo_ref,
                pl.program_id(0) % tiles_per_seq == 0)


def _mix_in(x, g, w_in, ln_g, ln_b, w_s, b_s_t, conv_w, seq, layer, name):
    rows, d = x.shape
    wd = ln_g.shape[2]
    tm = MIX_TM
    row_w = pl.BlockSpec((tm, wd), lambda i: (i, 0))
    return pl.pallas_call(
        functools.partial(_mix_in_kernel, tiles_per_seq=seq // tm),
        out_shape=(jax.ShapeDtypeStruct((rows, 3 * wd), BF16),
                   jax.ShapeDtypeStruct((rows, wd), BF16),
                   jax.ShapeDtypeStruct((rows, wd), BF16)),
        grid=(rows // tm,),
        in_specs=[
            pl.BlockSpec((tm, d), lambda i: (i, 0)),
            _layer_spec((1, d), layer),
            _layer_spec((d, 8 * wd), layer),
            _layer_spec((1, wd), layer),
            _layer_spec((1, wd), layer),
            _layer_spec((SGU_GROUPS, SGU_LEN, SGU_LEN), layer),
            _layer_spec((SGU_LEN, SGU_GROUPS), layer),
            _layer_spec((CONV_WIDTH, wd), layer),
        ],
        out_specs=(pl.BlockSpec((tm, 3 * wd), lambda i: (i, 0)), row_w, row_w),
        scratch_shapes=[pltpu.VMEM((CONV_HALO, wd), F32)],
        compiler_params=_params("arbitrary"),
        name=name,
    )(x, g, w_in, ln_g, ln_b, w_s, b_s_t, conv_w)


def _sb_attn_kernel(q_ref, k_ref, v_ref, tri_ref, o_ref, acc_ref, carry_ref):
    tq, kg = ATT_TQ, ATT_KG
    i = pl.program_id(1)
    n_pairs = q_ref.shape[1] // LANES
    lane = lax.broadcasted_iota(jnp.int32, (tq, LANES), 1)
    row = lax.broadcasted_iota(jnp.int32, (2 * tq, kg), 0)
    col = lax.broadcasted_iota(jnp.int32, (2 * tq, kg), 1)
    diag_mask = col < jnp.where(row >= tq, row - tq, row)

    def stacked_queries(p):
        q = q_ref[:, p * LANES:(p + 1) * LANES] * jnp.asarray(SB_HEAD_DIM ** -0.5, BF16)
        zero = jnp.zeros_like(q)
        return jnp.concatenate([jnp.where(lane < SB_HEAD_DIM, q, zero),
                                jnp.where(lane >= SB_HEAD_DIM, q, zero)], axis=0)

    def group(p, q2, g, mask=None):
        keys = pl.ds(pl.multiple_of(g * kg, kg), kg)
        z = _dot_nt(q2, k_ref[keys, p * LANES:(p + 1) * LANES])
        sp = jnp.log(1.0 + jnp.exp2(jnp.abs(z) * -LOG2E))
        log_sig = jnp.minimum(z, 0.0) - sp
        log_1m = log_sig - z
        if mask is not None:
            log_1m = jnp.where(mask, log_1m, 0.0)
            log_sig = jnp.where(mask, log_sig, MASKED_LOG_WEIGHT)
        carry = carry_ref[p]
        suffix = _dot(log_1m.astype(BF16), tri_ref[...])
        a = jnp.exp(log_sig + suffix + carry)
        acc_ref[p] += _dot(a.astype(BF16), v_ref[keys, p * LANES:(p + 1) * LANES])
        carry_ref[p] = carry + jnp.sum(log_1m, axis=-1, keepdims=True)

    acc_ref[...] = jnp.zeros_like(acc_ref)
    carry_ref[...] = jnp.zeros_like(carry_ref)
    q2s = [stacked_queries(p) for p in range(n_pairs)]
    for p in range(n_pairs):
        group(p, q2s[p], i, diag_mask)

    @pl.when(i >= 1)
    def _():
        for p in range(n_pairs):
            group(p, q2s[p], i - 1)

    def alive(p):
        return jnp.max(carry_ref[p]) > DEAD_LOG_WEIGHT

    @pl.when(jnp.logical_and(i >= 2, jnp.max(carry_ref[...]) > DEAD_LOG_WEIGHT))
    def _():
        for p in range(n_pairs):
            def step(g, p=p):
                group(p, q2s[p], g)
                return g - 1

            lax.while_loop(lambda g, p=p: jnp.logical_and(g >= 0, alive(p)), step, i - 2)

    for p in range(n_pairs):
        acc = acc_ref[p]
        o_ref[:, p * LANES:(p + 1) * LANES] = jnp.where(lane < SB_HEAD_DIM, acc[:tq], acc[tq:]).astype(o_ref.dtype)


def _tri_matrix():
    j = lax.broadcasted_iota(jnp.int32, (ATT_KG, ATT_KG), 0)
    s = lax.broadcasted_iota(jnp.int32, (ATT_KG, ATT_KG), 1)
    return (j > s).astype(BF16)


def _sb_attention(qkv, name):
    b, s, _ = qkv.shape
    tq = ATT_TQ
    width = SB_HEADS * SB_HEAD_DIM
    pairs = width // LANES
    return pl.pallas_call(
        _sb_attn_kernel,
        out_shape=jax.ShapeDtypeStruct((b, s, width), BF16),
        grid=(b, s // tq),
        in_specs=[
            pl.BlockSpec((None, tq, width), lambda bi, i: (bi, i, 0)),
            pl.BlockSpec((None, s, width), lambda bi, i: (bi, 0, 1)),
            pl.BlockSpec((None, s, width), lambda bi, i: (bi, 0, 2)),
            pl.BlockSpec((ATT_KG, ATT_KG), lambda bi, i: (0, 0)),
        ],
        out_specs=pl.BlockSpec((None, tq, width), lambda bi, i: (bi, i, 0)),
        scratch_shapes=[
            pltpu.VMEM((pairs, 2 * tq, LANES), F32),
            pltpu.VMEM((pairs, 2 * tq, 1), F32),
        ],
        compiler_params=_params("parallel", "arbitrary"),
        name=name,
    )(qkv, qkv, qkv, _tri_matrix())


def _merge_kernel(x_ref, g_ref, wga_ref, wgb_ref, wgc_ref, ya_ref, yb_ref, yc_ref, wb_ref, wo_ref, o_ref):
    x = x_ref[...]
    h = _rms(x, g_ref[...]).astype(BF16)
    merged = None
    for n, (wg_ref, y_ref) in enumerate(((wga_ref, ya_ref), (wgb_ref, yb_ref), (wgc_ref, yc_ref))):
        gate = jax.nn.sigmoid(_dot(h, wg_ref[...]))
        term = gate * _dot(y_ref[...], wb_ref[n])
        merged = term if merged is None else merged + term
    o_ref[...] = x + _dot(merged.astype(BF16), wo_ref[...])


def _merge(x, g, w_in, ya, yb, yc, w_branch, w_out, layer, name):
    rows, d = x.shape
    wd = ya.shape[1]
    tm = MERGE_TM
    first_gate_block = (w_in.shape[2] - N_BRANCHES * d) // d
    row_d = pl.BlockSpec((tm, d), lambda i: (i, 0))
    row_w = pl.BlockSpec((tm, wd), lambda i: (i, 0))
    gate_w = [pl.BlockSpec((None, d, d), lambda i, n=n: (layer, 0, first_gate_block + n))
              for n in range(N_BRANCHES)]
    return pl.pallas_call(
        _merge_kernel,
        out_shape=jax.ShapeDtypeStruct((rows, d), F32),
        grid=(rows // tm,),
        in_specs=[
            row_d,
            _layer_spec((1, d), layer),
            *gate_w,
            row_w, row_w, row_w,
            _layer_spec((N_BRANCHES, wd, d), layer),
            _layer_spec((d, d), layer),
        ],
        out_specs=row_d,
        compiler_params=_params("parallel"),
        name=name,
    )(x, g, w_in, w_in, w_in, ya, yb, yc, w_branch, w_out)


def _xattn_kernel(x_ref, g_ref, wq_ref, k_ref, v_ref, wo_ref, o_ref):
    d = x_ref.shape[1]
    dh = d // XA_HEADS
    x = x_ref[...]
    h = _rms(x, g_ref[...]).astype(BF16)
    q = (_dot(h, wq_ref[...]) * (dh ** -0.5)).astype(BF16)
    heads = []
    for hd in range(XA_HEADS):
        cols = slice(hd * dh, (hd + 1) * dh)
        s = _dot_nt(q[:, cols], k_ref[:, cols])
        p = jnp.exp(s - jnp.max(s, axis=-1, keepdims=True))
        p = p / jnp.sum(p, axis=-1, keepdims=True)
        heads.append(_dot(p.astype(BF16), v_ref[:, cols]).astype(BF16))
    o_ref[...] = x + _dot(jnp.concatenate(heads, axis=1), wo_ref[...])


def _xattn(x, g, wq, kv, wo, layer, name):
    b, s, d = x.shape
    mem = kv.shape[1]
    tm = XA_TM
    row_d = pl.BlockSpec((None, tm, d), lambda bi, i: (bi, i, 0))
    return pl.pallas_call(
        _xattn_kernel,
        out_shape=jax.ShapeDtypeStruct((b, s, d), F32),
        grid=(b, s // tm),
        in_specs=[
            row_d,
            _layer_spec((1, d), layer),
            _layer_spec((d, d), layer),
            pl.BlockSpec((None, mem, d), lambda bi, i: (bi, 0, 0)),
            pl.BlockSpec((None, mem, d), lambda bi, i: (bi, 0, 1)),
            _layer_spec((d, d), layer),
        ],
        out_specs=row_d,
        compiler_params=_params("parallel", "parallel"),
        name=name,
    )(x, g, wq, kv, kv, wo)


def _ffn_kernel(x_ref, g_ref, wg_ref, wu_ref, wd_ref, fg_ref, o_ref, h_ref, acc_ref, *, final_norm):
    j = pl.program_id(1)

    @pl.when(j == 0)
    def _():
        h_ref[...] = _rms(x_ref[...], g_ref[...]).astype(BF16)
        acc_ref[...] = jnp.zeros_like(acc_ref)

    h = h_ref[...]
    a = _dot(h, wg_ref[...])
    t = (a * jax.nn.sigmoid(a)) * _dot(h, wu_ref[...])
    acc_ref[...] += _dot(t.astype(BF16), wd_ref[...])

    @pl.when(j == pl.num_programs(1) - 1)
    def _():
        y = x_ref[...] + acc_ref[...]
        if final_norm:
            y = _rms(y, fg_ref[...])
        o_ref[...] = y


def _ffn(x, g, w_gate, w_up, w_down, final_g, final_norm, layer, name):
    rows, d = x.shape
    hid = w_gate.shape[2]
    tm, th = FFN_TM, FFN_TH
    row_d = pl.BlockSpec((tm, d), lambda i, j: (i, 0))
    return pl.pallas_call(
        functools.partial(_ffn_kernel, final_norm=final_norm),
        out_shape=jax.ShapeDtypeStruct((rows, d), F32),
        grid=(rows // tm, hid // th),
        in_specs=[
            row_d,
            _layer_spec((1, d), layer),
            pl.BlockSpec((None, d, th), lambda i, j: (layer, 0, j)),
            pl.BlockSpec((None, d, th), lambda i, j: (layer, 0, j)),
            pl.BlockSpec((None, th, d), lambda i, j: (layer, j, 0)),
            pl.BlockSpec((1, d), lambda i, j: (0, 0)),
        ],
        out_specs=row_d,
        scratch_shapes=[pltpu.VMEM((tm, d), BF16), pltpu.VMEM((tm, d), F32)],
        compiler_params=_params("parallel", "arbitrary"),
        name=name,
    )(x, g, w_gate, w_up, w_down, final_g)


def kernel(x, mem, norm_mix_g, w_in, sgu_ln_g, sgu_ln_b, w_spatial, b_spatial, conv_w, w_branch, w_out, norm_xa_g, mem_norm_g, w_q_xa, w_k_xa, w_v_xa, w_o_xa, norm_ffn_g, w_gate_ffn, w_up_ffn, w_down_ffn, final_g):
    b, s, d = x.shape
    depth = w_in.shape[0]
    wd = w_branch.shape[2]
    n_mem = mem.shape[1]
    assert d % LANES == 0 and s % max(ATT_TQ, MIX_TM, MERGE_TM, XA_TM) == 0
    assert ATT_KG == ATT_TQ and ATT_KG % LANES == 0 and MIX_TM % SGU_LEN == 0
    assert (b * s) % FFN_TM == 0 and wd == SB_HEADS * SB_HEAD_DIM
    assert w_gate_ffn.shape[2] % FFN_TH == 0 and w_in.shape[2] == 8 * wd + N_BRANCHES * d

    rows3 = lambda v: v.reshape(depth, 1, -1)
    w_in_bf = w_in.astype(BF16)
    w_branch_bf, w_out_bf = w_branch.astype(BF16), w_out.astype(BF16)
    w_q_bf, w_o_bf = w_q_xa.astype(BF16), w_o_xa.astype(BF16)
    w_kv_bf = jnp.concatenate([w_k_xa, w_v_xa], axis=2).astype(BF16)
    w_gate_bf, w_up_bf, w_down_bf = (w.astype(BF16) for w in (w_gate_ffn, w_up_ffn, w_down_ffn))
    g_mix, g_xa, g_mem, g_ffn = (rows3(g) for g in (norm_mix_g, norm_xa_g, mem_norm_g, norm_ffn_g))
    ln_g, ln_b = rows3(sgu_ln_g), rows3(sgu_ln_b)
    b_s_t = jnp.swapaxes(b_spatial, 1, 2)

    xf = x.reshape(b * s, d)
    memf = mem.reshape(b * n_mem, d)
    for l in range(depth):
        qkv, yb, yc = _mix_in(xf, g_mix, w_in_bf, ln_g, ln_b, w_spatial, b_s_t, conv_w, s, l, f"mix_in_{l}")
        ya = _sb_attention(qkv.reshape(b, s, 3 * wd), f"sb_attn_{l}").reshape(b * s, wd)
        xf = _merge(xf, g_mix, w_in_bf, ya, yb, yc, w_branch_bf, w_out_bf, l, f"merge_{l}")
        kv = _norm_matmul(memf, g_mem, w_kv_bf, l, f"mem_kv_{l}").reshape(b, n_mem, 2 * d)
        xf = _xattn(xf.reshape(b, s, d), g_xa, w_q_bf, kv, w_o_bf, l, f"xattn_{l}").reshape(b * s, d)
        xf = _ffn(xf, g_ffn, w_gate_bf, w_up_bf, w_down_bf, final_g.reshape(1, d), l == depth - 1, l, f"ffn_{l}")
    return xf.reshape(b, s, d)
```

```python
import functools

import jax
import jax.numpy as jnp
from jax import lax
from jax.experimental import pallas as pl
from jax.experimental.pallas import tpu as pltpu

F32 = jnp.float32
BF16 = jnp.bfloat16

LANES = 128
V7X_VMEM_LIMIT_BYTES = 56 * 1024 * 1024

SB_HEADS = 8
SB_HEAD_DIM = 64
SGU_LEN = 128
SGU_GROUPS = 4
CHUNK = 64
CONV_WIDTH = 3
N_BRANCHES = 3
XA_HEADS = 4
RMS_EPS = 1e-6
LN_EPS = 1e-5
LOG2E = 1.4426950408889634
MASKED_LOG_WEIGHT = -1e30
DEAD_LOG_WEIGHT = -120.0

PROJ_TM = 1024
PROJ_TN = 512
SUB_TM = 512
MIX_TM = 1024
ATT_TQ = 256
ATT_KG = 256
MERGE_TM = 1024
XA_TM = 2048
FFN_TM = 1024
FFN_TH = 256
CONV_HALO = 8


def _params(*semantics):
    return pltpu.CompilerParams(dimension_semantics=semantics,
                                vmem_limit_bytes=V7X_VMEM_LIMIT_BYTES)


def _layer_spec(shape, layer):
    zeros = (0,) * len(shape)
    return pl.BlockSpec((None, *shape), lambda *_: (layer, *zeros), pipeline_mode=pl.Buffered(1))


def _sub_tiles(n_rows):
    return [slice(r, r + SUB_TM) for r in range(0, n_rows, SUB_TM)]


def _rms(x, g):
    return x * lax.rsqrt(jnp.mean(x * x, axis=-1, keepdims=True) + RMS_EPS) * g


def _dot(a, b):
    return jnp.dot(a, b, preferred_element_type=F32)


def _dot_nt(a, b):
    return lax.dot_general(a, b, (((1,), (1,)), ((), ())), preferred_element_type=F32)


def _norm_matmul_kernel(x_ref, g_ref, w_ref, o_ref, h_ref):
    @pl.when(pl.program_id(1) == 0)
    def _():
        h_ref[...] = _rms(x_ref[...], g_ref[...]).astype(BF16)

    o_ref[...] = _dot(h_ref[...], w_ref[...]).astype(o_ref.dtype)


def _norm_matmul(x, g, w, layer, name):
    rows, d = x.shape
    cols = w.shape[2]
    tm = min(PROJ_TM, rows)
    return pl.pallas_call(
        _norm_matmul_kernel,
        out_shape=jax.ShapeDtypeStruct((rows, cols), BF16),
        grid=(rows // tm, cols // PROJ_TN),
        in_specs=[
            pl.BlockSpec((tm, d), lambda i, j: (i, 0)),
            _layer_spec((1, d), layer),
            pl.BlockSpec((None, d, PROJ_TN), lambda i, j: (layer, 0, j)),
        ],
        out_specs=pl.BlockSpec((tm, PROJ_TN), lambda i, j: (i, j)),
        scratch_shapes=[pltpu.VMEM((tm, d), BF16)],
        compiler_params=_params("parallel", "arbitrary"),
        name=name,
    )(x, g, w)


def _spatial_gating(z, lng_ref, lnb_ref, ws_ref, bs_ref, o_ref):
    tm = z.shape[0]
    wd = o_ref.shape[1]
    gd = wd // SGU_GROUPS
    z = 0.5 * z * (1.0 + lax.erf(z * (2.0 ** -0.5)))
    u = z[:, :wd]
    v = z[:, wd:]
    mu = jnp.mean(v, axis=-1, keepdims=True)
    vc = v - mu
    v = vc * lax.rsqrt(jnp.mean(vc * vc, axis=-1, keepdims=True) + LN_EPS)
    v = (v * lng_ref[...] + lnb_ref[...]).astype(BF16)
    pos_t = lax.broadcasted_iota(jnp.int32, (SGU_LEN, SGU_LEN), 0)
    pos_s = lax.broadcasted_iota(jnp.int32, (SGU_LEN, SGU_LEN), 1)
    allowed = (pos_t // CHUNK) >= (pos_s // CHUNK)
    for gi in range(SGU_GROUPS):
        wm = jnp.where(allowed, ws_ref[gi], 0.0).astype(BF16)
        bias = bs_ref[:, gi:gi + 1]
        for c in range(tm // SGU_LEN):
            rows = slice(c * SGU_LEN, (c + 1) * SGU_LEN)
            cols = slice(gi * gd, (gi + 1) * gd)
            vm = _dot(wm, v[rows, cols]) + bias
            o_ref[rows, cols] = (u[rows, cols] * vm).astype(o_ref.dtype)


def _gated_conv(p, cw_ref, o_ref, halo):
    tm = p.shape[0]
    wd = o_ref.shape[1]
    cb = p[:, :wd]
    y = p[:, wd:2 * wd] * p[:, 2 * wd:]
    prev1 = halo[CONV_HALO - 1:CONV_HALO, :]
    prev2 = halo[CONV_HALO - 2:CONV_HALO - 1, :]
    row = lax.broadcasted_iota(jnp.int32, (tm, wd), 0)
    y1 = jnp.where(row == 0, prev1, pltpu.roll(y, 1, axis=0))
    y2 = jnp.where(row == 0, prev2, jnp.where(row == 1, prev1, pltpu.roll(y, 2, axis=0)))
    cw = cw_ref[...]
    conv = cw[0:1, :] * y2 + cw[1:2, :] * y1 + cw[2:3, :] * y
    o_ref[...] = (cb * conv).astype(o_ref.dtype)
    return y[tm - CONV_HALO:, :]


def _mix_in_kernel(x_ref, g_ref, w_ref, lng_ref, lnb_ref, ws_ref, bs_ref, cw_ref,
                   qkv_ref, yb_ref, yc_ref, halo_ref, *, tiles_per_seq):
    wd = yb_ref.shape[1]
    c_qkv, c_sgu = 3 * wd, 5 * wd
    halo = jnp.where(pl.program_id(0) % tiles_per_seq == 0, 0.0, halo_ref[...])
    for rows in _sub_tiles(x_ref.shape[0]):
        h = _rms(x_ref[rows, :], g_ref[...]).astype(BF16)
        qkv_ref[rows, :] = _dot(h, w_ref[:, :c_qkv]).astype(qkv_ref.dtype)
        _spatial_gating(_dot(h, w_ref[:, c_qkv:c_sgu]), lng_ref, lnb_ref, ws_ref, bs_ref, yb_ref.at[rows, :])
        halo = _gated_conv(_dot(h, w_ref[:, c_sgu:]), cw_ref, yc_ref.at[rows, :], halo)
    halo_ref[...] = halo


def _mix_in(x, g, w_in, ln_g, ln_b, w_s, b_s_t, conv_w, seq, layer, name):
    rows, d = x.shape
    wd = ln_g.shape[2]
    tm = MIX_TM
    row_w = pl.BlockSpec((tm, wd), lambda i: (i, 0))
    return pl.pallas_call(
        functools.partial(_mix_in_kernel, tiles_per_seq=seq // tm),
        out_shape=(jax.ShapeDtypeStruct((rows, 3 * wd), BF16),
                   jax.ShapeDtypeStruct((rows, wd), BF16),
                   jax.ShapeDtypeStruct((rows, wd), BF16)),
        grid=(rows // tm,),
        in_specs=[
            pl.BlockSpec((tm, d), lambda i: (i, 0)),
            _layer_spec((1, d), layer),
            _layer_spec((d, 8 * wd), layer),
            _layer_spec((1, wd), layer),
            _layer_spec((1, wd), layer),
            _layer_spec((SGU_GROUPS, SGU_LEN, SGU_LEN), layer),
            _layer_spec((SGU_LEN, SGU_GROUPS), layer),
            _layer_spec((CONV_WIDTH, wd), layer),
        ],
        out_specs=(pl.BlockSpec((tm, 3 * wd), lambda i: (i, 0)), row_w, row_w),
        scratch_shapes=[pltpu.VMEM((CONV_HALO, wd), F32)],
        compiler_params=_params("arbitrary"),
        name=name,
    )(x, g, w_in, ln_g, ln_b, w_s, b_s_t, conv_w)


def _sb_attn_kernel(q_ref, k_ref, v_ref, tri_ref, o_ref, acc_ref, carry_ref):
    tq, kg = ATT_TQ, ATT_KG
    i = pl.program_id(1)
    n_pairs = q_ref.shape[1] // LANES
    lane = lax.broadcasted_iota(jnp.int32, (tq, LANES), 1)
    row = lax.broadcasted_iota(jnp.int32, (2 * tq, kg), 0)
    col = lax.broadcasted_iota(jnp.int32, (2 * tq, kg), 1)
    diag_mask = col < jnp.where(row >= tq, row - tq, row)

    def stacked_queries(p):
        q = q_ref[:, p * LANES:(p + 1) * LANES] * jnp.asarray(SB_HEAD_DIM ** -0.5, BF16)
        zero = jnp.zeros_like(q)
        return jnp.concatenate([jnp.where(lane < SB_HEAD_DIM, q, zero),
                                jnp.where(lane >= SB_HEAD_DIM, q, zero)], axis=0)

    def group(p, q2, g, mask=None):
        keys = pl.ds(pl.multiple_of(g * kg, kg), kg)
        z = _dot_nt(q2, k_ref[keys, p * LANES:(p + 1) * LANES])
        sp = jnp.log(1.0 + jnp.exp2(jnp.abs(z) * -LOG2E))
        log_sig = jnp.minimum(z, 0.0) - sp
        log_1m = log_sig - z
        if mask is not None:
            log_1m = jnp.where(mask, log_1m, 0.0)
            log_sig = jnp.where(mask, log_sig, MASKED_LOG_WEIGHT)
        carry = carry_ref[p]
        suffix = _dot(log_1m.astype(BF16), tri_ref[...])
        a = jnp.exp(log_sig + suffix + carry)
        acc_ref[p] += _dot(a.astype(BF16), v_ref[keys, p * LANES:(p + 1) * LANES])
        carry_ref[p] = carry + jnp.sum(log_1m, axis=-1, keepdims=True)

    acc_ref[...] = jnp.zeros_like(acc_ref)
    carry_ref[...] = jnp.zeros_like(carry_ref)
    q2s = [stacked_queries(p) for p in range(n_pairs)]
    for p in range(n_pairs):
        group(p, q2s[p], i, diag_mask)

    @pl.when(i >= 1)
    def _():
        for p in range(n_pairs):
            group(p, q2s[p], i - 1)

    def alive(p):
        return jnp.max(carry_ref[p]) > DEAD_LOG_WEIGHT

    @pl.when(jnp.logical_and(i >= 2, jnp.max(carry_ref[...]) > DEAD_LOG_WEIGHT))
    def _():
        for p in range(n_pairs):
            def step(g, p=p):
                group(p, q2s[p], g)
                return g - 1

            lax.while_loop(lambda g, p=p: jnp.logical_and(g >= 0, alive(p)), step, i - 2)

    for p in range(n_pairs):
        acc = acc_ref[p]
        o_ref[:, p * LANES:(p + 1) * LANES] = jnp.where(lane < SB_HEAD_DIM, acc[:tq], acc[tq:]).astype(o_ref.dtype)


def _tri_matrix():
    j = lax.broadcasted_iota(jnp.int32, (ATT_KG, ATT_KG), 0)
    s = lax.broadcasted_iota(jnp.int32, (ATT_KG, ATT_KG), 1)
    return (j > s).astype(BF16)


def _sb_attention(qkv, name):
    b, s, _ = qkv.shape
    tq = ATT_TQ
    width = SB_HEADS * SB_HEAD_DIM
    pairs = width // LANES
    return pl.pallas_call(
        _sb_attn_kernel,
        out_shape=jax.ShapeDtypeStruct((b, s, width), BF16),
        grid=(b, s // tq),
        in_specs=[
            pl.BlockSpec((None, tq, width), lambda bi, i: (bi, i, 0)),
            pl.BlockSpec((None, s, width), lambda bi, i: (bi, 0, 1)),
            pl.BlockSpec((None, s, width), lambda bi, i: (bi, 0, 2)),
            pl.BlockSpec((ATT_KG, ATT_KG), lambda bi, i: (0, 0)),
        ],
        out_specs=pl.BlockSpec((None, tq, width), lambda bi, i: (bi, i, 0)),
        scratch_shapes=[
            pltpu.VMEM((pairs, 2 * tq, LANES), F32),
            pltpu.VMEM((pairs, 2 * tq, 1), F32),
        ],
        compiler_params=_params("parallel", "arbitrary"),
        name=name,
    )(qkv, qkv, qkv, _tri_matrix())


def _merge_kernel(x_ref, g_ref, wga_ref, wgb_ref, wgc_ref, ya_ref, yb_ref, yc_ref, wb_ref, wo_ref, o_ref):
    for rows in _sub_tiles(x_ref.shape[0]):
        x = x_ref[rows, :]
        h = _rms(x, g_ref[...]).astype(BF16)
        merged = None
        for n, (wg_ref, y_ref) in enumerate(((wga_ref, ya_ref), (wgb_ref, yb_ref), (wgc_ref, yc_ref))):
            gate = jax.nn.sigmoid(_dot(h, wg_ref[...]))
            term = gate * _dot(y_ref[rows, :], wb_ref[n])
            merged = term if merged is None else merged + term
        o_ref[rows, :] = x + _dot(merged.astype(BF16), wo_ref[...])


def _merge(x, g, w_in, ya, yb, yc, w_branch, w_out, layer, name):
    rows, d = x.shape
    wd = ya.shape[1]
    tm = MERGE_TM
    first_gate_block = (w_in.shape[2] - N_BRANCHES * d) // d
    row_d = pl.BlockSpec((tm, d), lambda i: (i, 0))
    row_w = pl.BlockSpec((tm, wd), lambda i: (i, 0))
    gate_w = [pl.BlockSpec((None, d, d), lambda i, n=n: (layer, 0, first_gate_block + n),
                           pipeline_mode=pl.Buffered(1)) for n in range(N_BRANCHES)]
    return pl.pallas_call(
        _merge_kernel,
        out_shape=jax.ShapeDtypeStruct((rows, d), F32),
        grid=(rows // tm,),
        in_specs=[
            row_d,
            _layer_spec((1, d), layer),
            *gate_w,
            row_w, row_w, row_w,
            _layer_spec((N_BRANCHES, wd, d), layer),
            _layer_spec((d, d), layer),
        ],
        out_specs=row_d,
        compiler_params=_params("parallel"),
        name=name,
    )(x, g, w_in, w_in, w_in, ya, yb, yc, w_branch, w_out)


def _xattn_kernel(x_ref, g_ref, wq_ref, k_ref, v_ref, wo_ref, o_ref):
    d = x_ref.shape[1]
    dh = d // XA_HEADS
    for rows in _sub_tiles(x_ref.shape[0]):
        x = x_ref[rows, :]
        h = _rms(x, g_ref[...]).astype(BF16)
        q = (_dot(h, wq_ref[...]) * (dh ** -0.5)).astype(BF16)
        heads = []
        for hd in range(XA_HEADS):
            cols = slice(hd * dh, (hd + 1) * dh)
            s = _dot_nt(q[:, cols], k_ref[:, cols])
            p = jnp.exp(s - jnp.max(s, axis=-1, keepdims=True))
            p = p / jnp.sum(p, axis=-1, keepdims=True)
            heads.append(_dot(p.astype(BF16), v_ref[:, cols]).astype(BF16))
        o_ref[rows, :] = x + _dot(jnp.concatenate(heads, axis=1), wo_ref[...])


def _xattn(x, g, wq, kv, wo, layer, name):
    b, s, d = x.shape
    mem = kv.shape[1]
    tm = XA_TM
    row_d = pl.BlockSpec((None, tm, d), lambda bi, i: (bi, i, 0))
    return pl.pallas_call(
        _xattn_kernel,
        out_shape=jax.ShapeDtypeStruct((b, s, d), F32),
        grid=(b, s // tm),
        in_specs=[
            row_d,
            _layer_spec((1, d), layer),
            _layer_spec((d, d), layer),
            pl.BlockSpec((None, mem, d), lambda bi, i: (bi, 0, 0)),
            pl.BlockSpec((None, mem, d), lambda bi, i: (bi, 0, 1)),
            _layer_spec((d, d), layer),
        ],
        out_specs=row_d,
        compiler_params=_params("parallel", "parallel"),
        name=name,
    )(x, g, wq, kv, kv, wo)


def _ffn_kernel(x_ref, g_ref, wg_ref, wu_ref, wd_ref, fg_ref, o_ref, h_ref, acc_ref, *, final_norm):
    subs = _sub_tiles(x_ref.shape[0])
    for rows in subs:
        h_ref[rows, :] = _rms(x_ref[rows, :], g_ref[...]).astype(BF16)

    def chunk(c, first):
        for rows in subs:
            h = h_ref[rows, :]
            a = _dot(h, wg_ref[c])
            t = ((a * jax.nn.sigmoid(a)) * _dot(h, wu_ref[c])).astype(BF16)
            if first:
                acc_ref[rows, :] = _dot(t, wd_ref[c])
            else:
                acc_ref[rows, :] += _dot(t, wd_ref[c])

    chunk(0, True)

    def body(c, carry):
        chunk(c, False)
        return carry

    lax.fori_loop(1, wg_ref.shape[0], body, 0)
    for rows in subs:
        y = x_ref[rows, :] + acc_ref[rows, :]
        if final_norm:
            y = _rms(y, fg_ref[...])
        o_ref[rows, :] = y


def _ffn(x, g, w_gate, w_up, w_down, final_g, final_norm, layer, name):
    rows, d = x.shape
    n_chunks, _, th = w_gate.shape[1:]
    tm = FFN_TM
    row_d = pl.BlockSpec((tm, d), lambda i: (i, 0))
    return pl.pallas_call(
        functools.partial(_ffn_kernel, final_norm=final_norm),
        out_shape=jax.ShapeDtypeStruct((rows, d), F32),
        grid=(rows // tm,),
        in_specs=[
            row_d,
            _layer_spec((1, d), layer),
            _layer_spec((n_chunks, d, th), layer),
            _layer_spec((n_chunks, d, th), layer),
            _layer_spec((n_chunks, th, d), layer),
            pl.BlockSpec((1, d), lambda i: (0, 0)),
        ],
        out_specs=row_d,
        scratch_shapes=[pltpu.VMEM((tm, d), BF16), pltpu.VMEM((tm, d), F32)],
        compiler_params=_params("parallel"),
        name=name,
    )(x, g, w_gate, w_up, w_down, final_g)


def kernel(x, mem, norm_mix_g, w_in, sgu_ln_g, sgu_ln_b, w_spatial, b_spatial, conv_w, w_branch, w_out, norm_xa_g, mem_norm_g, w_q_xa, w_k_xa, w_v_xa, w_o_xa, norm_ffn_g, w_gate_ffn, w_up_ffn, w_down_ffn, final_g):
    b, s, d = x.shape
    depth = w_in.shape[0]
    wd = w_branch.shape[2]
    n_mem = mem.shape[1]
    hid = w_gate_ffn.shape[2]
    assert d % LANES == 0 and s % max(ATT_TQ, MIX_TM, MERGE_TM, XA_TM) == 0
    assert ATT_KG == ATT_TQ and ATT_KG % LANES == 0 and SUB_TM % SGU_LEN == 0
    assert all(t % SUB_TM == 0 for t in (MIX_TM, MERGE_TM, XA_TM, FFN_TM))
    assert (b * s) % FFN_TM == 0 and wd == SB_HEADS * SB_HEAD_DIM
    assert hid % FFN_TH == 0 and w_in.shape[2] == 8 * wd + N_BRANCHES * d

    rows3 = lambda v: v.reshape(depth, 1, -1)
    w_in_bf = w_in.astype(BF16)
    w_branch_bf, w_out_bf = w_branch.astype(BF16), w_out.astype(BF16)
    w_q_bf, w_o_bf = w_q_xa.astype(BF16), w_o_xa.astype(BF16)
    w_kv_bf = jnp.concatenate([w_k_xa, w_v_xa], axis=2).astype(BF16)
    w_gate_bf, w_up_bf = (w.reshape(depth, d, hid // FFN_TH, FFN_TH).transpose(0, 2, 1, 3).astype(BF16)
                          for w in (w_gate_ffn, w_up_ffn))
    w_down_bf = w_down_ffn.reshape(depth, hid // FFN_TH, FFN_TH, d).astype(BF16)
    g_mix, g_xa, g_mem, g_ffn = (rows3(g) for g in (norm_mix_g, norm_xa_g, mem_norm_g, norm_ffn_g))
    ln_g, ln_b = rows3(sgu_ln_g), rows3(sgu_ln_b)
    b_s_t = jnp.swapaxes(b_spatial, 1, 2)

    xf = x.reshape(b * s, d)
    memf = mem.reshape(b * n_mem, d)
    for l in range(depth):
        qkv, yb, yc = _mix_in(xf, g_mix, w_in_bf, ln_g, ln_b, w_spatial, b_s_t, conv_w, s, l, f"mix_in_{l}")
        ya = _sb_attention(qkv.reshape(b, s, 3 * wd), f"sb_attn_{l}").reshape(b * s, wd)
        xf = _merge(xf, g_mix, w_in_bf, ya, yb, yc, w_branch_bf, w_out_bf, l, f"merge_{l}")
        kv = _norm_matmul(memf, g_mem, w_kv_bf, l, f"mem_kv_{l}").reshape(b, n_mem, 2 * d)
        xf = _xattn(xf.reshape(b, s, d), g_xa, w_q_bf, kv, w_o_bf, l, f"xattn_{l}").reshape(b * s, d)
        xf = _ffn(xf, g_ffn, w_gate_bf, w_up_bf, w_down_bf, final_g.reshape(1, d), l == depth - 1, l, f"ffn_{l}")
    return xf.reshape(b, s, d)
```

```python
import functools

import jax
import jax.numpy as jnp
from jax import lax
from jax.experimental import pallas as pl
from jax.experimental.pallas import tpu as pltpu

F32 = jnp.float32
BF16 = jnp.bfloat16

LANES = 128
V7X_VMEM_LIMIT_BYTES = 56 * 1024 * 1024

SB_HEADS = 8
SB_HEAD_DIM = 64
SGU_LEN = 128
SGU_GROUPS = 4
CHUNK = 64
CONV_WIDTH = 3
N_BRANCHES = 3
XA_HEADS = 4
RMS_EPS = 1e-6
LN_EPS = 1e-5
LOG2E = 1.4426950408889634
MASKED_LOG_WEIGHT = -1e30
DEAD_LOG_WEIGHT = -120.0

PROJ_TM = 1024
PROJ_TN = 512
SUB_TM = 512
MIX_TM = 1024
ATT_TQ = 256
ATT_KG = 256
MERGE_TM = 1024
XA_TM = 2048
FFN_TM = 1024
FFN_TH = 256
CONV_HALO = 8


def _params(*semantics):
    return pltpu.CompilerParams(dimension_semantics=semantics,
                                vmem_limit_bytes=V7X_VMEM_LIMIT_BYTES)


def _layer_spec(shape, layer):
    zeros = (0,) * len(shape)
    return pl.BlockSpec((None, *shape), lambda *_: (layer, *zeros), pipeline_mode=pl.Buffered(1))


def _sub_tiles(n_rows):
    return [slice(r, r + SUB_TM) for r in range(0, n_rows, SUB_TM)]


def _rms(x, g):
    return x * lax.rsqrt(jnp.mean(x * x, axis=-1, keepdims=True) + RMS_EPS) * g


def _dot(a, b):
    return jnp.dot(a, b, preferred_element_type=F32)


def _dot_nt(a, b):
    return lax.dot_general(a, b, (((1,), (1,)), ((), ())), preferred_element_type=F32)


def _norm_matmul_kernel(x_ref, g_ref, w_ref, o_ref, h_ref):
    @pl.when(pl.program_id(1) == 0)
    def _():
        h_ref[...] = _rms(x_ref[...], g_ref[...]).astype(BF16)

    o_ref[...] = _dot(h_ref[...], w_ref[...]).astype(o_ref.dtype)


def _norm_matmul(x, g, w, layer, name):
    rows, d = x.shape
    cols = w.shape[2]
    tm = min(PROJ_TM, rows)
    return pl.pallas_call(
        _norm_matmul_kernel,
        out_shape=jax.ShapeDtypeStruct((rows, cols), BF16),
        grid=(rows // tm, cols // PROJ_TN),
        in_specs=[
            pl.BlockSpec((tm, d), lambda i, j: (i, 0)),
            _layer_spec((1, d), layer),
            pl.BlockSpec((None, d, PROJ_TN), lambda i, j: (layer, 0, j)),
        ],
        out_specs=pl.BlockSpec((tm, PROJ_TN), lambda i, j: (i, j)),
        scratch_shapes=[pltpu.VMEM((tm, d), BF16)],
        compiler_params=_params("parallel", "arbitrary"),
        name=name,
    )(x, g, w)


def _spatial_gating(z, lng_ref, lnb_ref, ws_ref, bs_ref, o_ref):
    tm = z.shape[0]
    wd = o_ref.shape[1]
    gd = wd // SGU_GROUPS
    z = 0.5 * z * (1.0 + lax.erf(z * (2.0 ** -0.5)))
    u = z[:, :wd]
    v = z[:, wd:]
    mu = jnp.mean(v, axis=-1, keepdims=True)
    vc = v - mu
    v = vc * lax.rsqrt(jnp.mean(vc * vc, axis=-1, keepdims=True) + LN_EPS)
    v = (v * lng_ref[...] + lnb_ref[...]).astype(BF16)
    pos_t = lax.broadcasted_iota(jnp.int32, (SGU_LEN, SGU_LEN), 0)
    pos_s = lax.broadcasted_iota(jnp.int32, (SGU_LEN, SGU_LEN), 1)
    allowed = (pos_t // CHUNK) >= (pos_s // CHUNK)
    for gi in range(SGU_GROUPS):
        wm = jnp.where(allowed, ws_ref[gi], 0.0).astype(BF16)
        bias = bs_ref[:, gi:gi + 1]
        for c in range(tm // SGU_LEN):
            rows = slice(c * SGU_LEN, (c + 1) * SGU_LEN)
            cols = slice(gi * gd, (gi + 1) * gd)
            vm = _dot(wm, v[rows, cols]) + bias
            o_ref[rows, cols] = (u[rows, cols] * vm).astype(o_ref.dtype)


def _gated_conv(p, cw_ref, o_ref, halo):
    tm = p.shape[0]
    wd = o_ref.shape[1]
    cb = p[:, :wd]
    y = p[:, wd:2 * wd] * p[:, 2 * wd:]
    prev1 = halo[CONV_HALO - 1:CONV_HALO, :]
    prev2 = halo[CONV_HALO - 2:CONV_HALO - 1, :]
    row = lax.broadcasted_iota(jnp.int32, (tm, wd), 0)
    y1 = jnp.where(row == 0, prev1, pltpu.roll(y, 1, axis=0))
    y2 = jnp.where(row == 0, prev2, jnp.where(row == 1, prev1, pltpu.roll(y, 2, axis=0)))
    cw = cw_ref[...]
    conv = cw[0:1, :] * y2 + cw[1:2, :] * y1 + cw[2:3, :] * y
    o_ref[...] = (cb * conv).astype(o_ref.dtype)
    return y[tm - CONV_HALO:, :]


def _mix_in_kernel(x_ref, g_ref, w_ref, lng_ref, lnb_ref, ws_ref, bs_ref, cw_ref,
                   qkv_ref, yb_ref, yc_ref, halo_ref, *, tiles_per_seq):
    wd = yb_ref.shape[1]
    c_qkv, c_sgu = 3 * wd, 5 * wd
    halo = jnp.where(pl.program_id(0) % tiles_per_seq == 0, 0.0, halo_ref[...])
    subs = _sub_tiles(x_ref.shape[0])
    proj = []
    for rows in subs:
        h = _rms(x_ref[rows, :], g_ref[...]).astype(BF16)
        qkv_ref[rows, :] = _dot(h, w_ref[:, :c_qkv]).astype(qkv_ref.dtype)
        proj.append((_dot(h, w_ref[:, c_qkv:c_sgu]), _dot(h, w_ref[:, c_sgu:])))
    for rows, (z, p) in zip(subs, proj):
        _spatial_gating(z, lng_ref, lnb_ref, ws_ref, bs_ref, yb_ref.at[rows, :])
        halo = _gated_conv(p, cw_ref, yc_ref.at[rows, :], halo)
    halo_ref[...] = halo


def _mix_in(x, g, w_in, ln_g, ln_b, w_s, b_s_t, conv_w, seq, layer, name):
    rows, d = x.shape
    wd = ln_g.shape[2]
    tm = MIX_TM
    row_w = pl.BlockSpec((tm, wd), lambda i: (i, 0))
    return pl.pallas_call(
        functools.partial(_mix_in_kernel, tiles_per_seq=seq // tm),
        out_shape=(jax.ShapeDtypeStruct((rows, 3 * wd), BF16),
                   jax.ShapeDtypeStruct((rows, wd), BF16),
                   jax.ShapeDtypeStruct((rows, wd), BF16)),
        grid=(rows // tm,),
        in_specs=[
            pl.BlockSpec((tm, d), lambda i: (i, 0)),
            _layer_spec((1, d), layer),
            _layer_spec((d, 8 * wd), layer),
            _layer_spec((1, wd), layer),
            _layer_spec((1, wd), layer),
            _layer_spec((SGU_GROUPS, SGU_LEN, SGU_LEN), layer),
            _layer_spec((SGU_LEN, SGU_GROUPS), layer),
            _layer_spec((CONV_WIDTH, wd), layer),
        ],
        out_specs=(pl.BlockSpec((tm, 3 * wd), lambda i: (i, 0)), row_w, row_w),
        scratch_shapes=[pltpu.VMEM((CONV_HALO, wd), F32)],
        compiler_params=_params("arbitrary"),
        name=name,
    )(x, g, w_in, ln_g, ln_b, w_s, b_s_t, conv_w)


def _sb_attn_kernel(q_ref, k_ref, v_ref, tri_ref, o_ref, acc_ref, carry_ref):
    tq, kg = ATT_TQ, ATT_KG
    i = pl.program_id(1)
    n_pairs = q_ref.shape[1] // LANES
    lane = lax.broadcasted_iota(jnp.int32, (tq, LANES), 1)
    row = lax.broadcasted_iota(jnp.int32, (2 * tq, kg), 0)
    col = lax.broadcasted_iota(jnp.int32, (2 * tq, kg), 1)
    diag_mask = col < jnp.where(row >= tq, row - tq, row)

    def stacked_queries(p):
        q = q_ref[:, p * LANES:(p + 1) * LANES] * jnp.asarray(SB_HEAD_DIM ** -0.5, BF16)
        zero = jnp.zeros_like(q)
        return jnp.concatenate([jnp.where(lane < SB_HEAD_DIM, q, zero),
                                jnp.where(lane >= SB_HEAD_DIM, q, zero)], axis=0)

    def group(p, q2, g, mask=None):
        keys = pl.ds(pl.multiple_of(g * kg, kg), kg)
        z = _dot_nt(q2, k_ref[keys, p * LANES:(p + 1) * LANES])
        sp = jnp.log(1.0 + jnp.exp2(jnp.abs(z) * -LOG2E))
        log_sig = jnp.minimum(z, 0.0) - sp
        log_1m = log_sig - z
        if mask is not None:
            log_1m = jnp.where(mask, log_1m, 0.0)
            log_sig = jnp.where(mask, log_sig, MASKED_LOG_WEIGHT)
        carry = carry_ref[p]
        suffix = _dot(log_1m.astype(BF16), tri_ref[...])
        a = jnp.exp(log_sig + suffix + carry)
        acc_ref[p] += _dot(a.astype(BF16), v_ref[keys, p * LANES:(p + 1) * LANES])
        carry_ref[p] = carry + jnp.sum(log_1m, axis=-1, keepdims=True)

    acc_ref[...] = jnp.zeros_like(acc_ref)
    carry_ref[...] = jnp.zeros_like(carry_ref)
    q2s = [stacked_queries(p) for p in range(n_pairs)]
    for p in range(n_pairs):
        group(p, q2s[p], i, diag_mask)

    @pl.when(i >= 1)
    def _():
        for p in range(n_pairs):
            group(p, q2s[p], i - 1)

    def alive(p):
        return jnp.max(carry_ref[p]) > DEAD_LOG_WEIGHT

    @pl.when(jnp.logical_and(i >= 2, jnp.max(carry_ref[...]) > DEAD_LOG_WEIGHT))
    def _():
        for p in range(n_pairs):
            def step(g, p=p):
                group(p, q2s[p], g)
                return g - 1

            lax.while_loop(lambda g, p=p: jnp.logical_and(g >= 0, alive(p)), step, i - 2)

    for p in range(n_pairs):
        acc = acc_ref[p]
        o_ref[:, p * LANES:(p + 1) * LANES] = jnp.where(lane < SB_HEAD_DIM, acc[:tq], acc[tq:]).astype(o_ref.dtype)


def _tri_matrix():
    j = lax.broadcasted_iota(jnp.int32, (ATT_KG, ATT_KG), 0)
    s = lax.broadcasted_iota(jnp.int32, (ATT_KG, ATT_KG), 1)
    return (j > s).astype(BF16)


def _sb_attention(qkv, name):
    b, s, _ = qkv.shape
    tq = ATT_TQ
    width = SB_HEADS * SB_HEAD_DIM
    pairs = width // LANES
    return pl.pallas_call(
        _sb_attn_kernel,
        out_shape=jax.ShapeDtypeStruct((b, s, width), BF16),
        grid=(b, s // tq),
        in_specs=[
            pl.BlockSpec((None, tq, width), lambda bi, i: (bi, i, 0)),
            pl.BlockSpec((None, s, width), lambda bi, i: (bi, 0, 1)),
            pl.BlockSpec((None, s, width), lambda bi, i: (bi, 0, 2)),
            pl.BlockSpec((ATT_KG, ATT_KG), lambda bi, i: (0, 0)),
        ],
        out_specs=pl.BlockSpec((None, tq, width), lambda bi, i: (bi, i, 0)),
        scratch_shapes=[
            pltpu.VMEM((pairs, 2 * tq, LANES), F32),
            pltpu.VMEM((pairs, 2 * tq, 1), F32),
        ],
        compiler_params=_params("parallel", "arbitrary"),
        name=name,
    )(qkv, qkv, qkv, _tri_matrix())


def _merge_kernel(x_ref, g_ref, wga_ref, wgb_ref, wgc_ref, ya_ref, yb_ref, yc_ref, wb_ref, wo_ref, o_ref):
    for rows in _sub_tiles(x_ref.shape[0]):
        x = x_ref[rows, :]
        h = _rms(x, g_ref[...]).astype(BF16)
        merged = None
        for n, (wg_ref, y_ref) in enumerate(((wga_ref, ya_ref), (wgb_ref, yb_ref), (wgc_ref, yc_ref))):
            gate = jax.nn.sigmoid(_dot(h, wg_ref[...]))
            term = gate * _dot(y_ref[rows, :], wb_ref[n])
            merged = term if merged is None else merged + term
        o_ref[rows, :] = x + _dot(merged.astype(BF16), wo_ref[...])


def _merge(x, g, w_in, ya, yb, yc, w_branch, w_out, layer, name):
    rows, d = x.shape
    wd = ya.shape[1]
    tm = MERGE_TM
    first_gate_block = (w_in.shape[2] - N_BRANCHES * d) // d
    row_d = pl.BlockSpec((tm, d), lambda i: (i, 0))
    row_w = pl.BlockSpec((tm, wd), lambda i: (i, 0))
    gate_w = [pl.BlockSpec((None, d, d), lambda i, n=n: (layer, 0, first_gate_block + n),
                           pipeline_mode=pl.Buffered(1)) for n in range(N_BRANCHES)]
    return pl.pallas_call(
        _merge_kernel,
        out_shape=jax.ShapeDtypeStruct((rows, d), F32),
        grid=(rows // tm,),
        in_specs=[
            row_d,
            _layer_spec((1, d), layer),
            *gate_w,
            row_w, row_w, row_w,
            _layer_spec((N_BRANCHES, wd, d), layer),
            _layer_spec((d, d), layer),
        ],
        out_specs=row_d,
        compiler_params=_params("parallel"),
        name=name,
    )(x, g, w_in, w_in, w_in, ya, yb, yc, w_branch, w_out)


def _xattn_kernel(x_ref, g_ref, wq_ref, k_ref, v_ref, wo_ref, o_ref):
    d = x_ref.shape[1]
    dh = d // XA_HEADS
    for rows in _sub_tiles(x_ref.shape[0]):
        x = x_ref[rows, :]
        h = _rms(x, g_ref[...]).astype(BF16)
        q = (_dot(h, wq_ref[...]) * (dh ** -0.5)).astype(BF16)
        heads = []
        for hd in range(XA_HEADS):
            cols = slice(hd * dh, (hd + 1) * dh)
            s = _dot_nt(q[:, cols], k_ref[:, cols])
            p = jnp.exp(s - jnp.max(s, axis=-1, keepdims=True))
            p = p / jnp.sum(p, axis=-1, keepdims=True)
            heads.append(_dot(p.astype(BF16), v_ref[:, cols]).astype(BF16))
        o_ref[rows, :] = x + _dot(jnp.concatenate(heads, axis=1), wo_ref[...])


def _xattn(x, g, wq, kv, wo, layer, name):
    b, s, d = x.shape
    mem = kv.shape[1]
    tm = XA_TM
    row_d = pl.BlockSpec((None, tm, d), lambda bi, i: (bi, i, 0))
    return pl.pallas_call(
        _xattn_kernel,
        out_shape=jax.ShapeDtypeStruct((b, s, d), F32),
        grid=(b, s // tm),
        in_specs=[
            row_d,
            _layer_spec((1, d), layer),
            _layer_spec((d, d), layer),
            pl.BlockSpec((None, mem, d), lambda bi, i: (bi, 0, 0)),
            pl.BlockSpec((None, mem, d), lambda bi, i: (bi, 0, 1)),
            _layer_spec((d, d), layer),
        ],
        out_specs=row_d,
        compiler_params=_params("parallel", "parallel"),
        name=name,
    )(x, g, wq, kv, kv, wo)


def _ffn_kernel(x_ref, g_ref, wg_ref, wu_ref, wd_ref, fg_ref, o_ref, h_ref, acc_ref, *, final_norm):
    subs = _sub_tiles(x_ref.shape[0])
    for rows in subs:
        h_ref[rows, :] = _rms(x_ref[rows, :], g_ref[...]).astype(BF16)
    for c in range(0, wg_ref.shape[1], FFN_TH):
        cols = slice(c, c + FFN_TH)
        for rows in subs:
            h = h_ref[rows, :]
            a = _dot(h, wg_ref[:, cols])
            t = ((a * jax.nn.sigmoid(a)) * _dot(h, wu_ref[:, cols])).astype(BF16)
            if c == 0:
                acc_ref[rows, :] = _dot(t, wd_ref[cols, :])
            else:
                acc_ref[rows, :] += _dot(t, wd_ref[cols, :])
    for rows in subs:
        y = x_ref[rows, :] + acc_ref[rows, :]
        if final_norm:
            y = _rms(y, fg_ref[...])
        o_ref[rows, :] = y


def _ffn(x, g, w_gate, w_up, w_down, final_g, final_norm, layer, name):
    rows, d = x.shape
    hid = w_gate.shape[2]
    tm = FFN_TM
    row_d = pl.BlockSpec((tm, d), lambda i: (i, 0))
    return pl.pallas_call(
        functools.partial(_ffn_kernel, final_norm=final_norm),
        out_shape=jax.ShapeDtypeStruct((rows, d), F32),
        grid=(rows // tm,),
        in_specs=[
            row_d,
            _layer_spec((1, d), layer),
            _layer_spec((d, hid), layer),
            _layer_spec((d, hid), layer),
            _layer_spec((hid, d), layer),
            pl.BlockSpec((1, d), lambda i: (0, 0)),
        ],
        out_specs=row_d,
        scratch_shapes=[pltpu.VMEM((tm, d), BF16), pltpu.VMEM((tm, d), F32)],
        compiler_params=_params("parallel"),
        name=name,
    )(x, g, w_gate, w_up, w_down, final_g)


def kernel(x, mem, norm_mix_g, w_in, sgu_ln_g, sgu_ln_b, w_spatial, b_spatial, conv_w, w_branch, w_out, norm_xa_g, mem_norm_g, w_q_xa, w_k_xa, w_v_xa, w_o_xa, norm_ffn_g, w_gate_ffn, w_up_ffn, w_down_ffn, final_g):
    b, s, d = x.shape
    depth = w_in.shape[0]
    wd = w_branch.shape[2]
    n_mem = mem.shape[1]
    hid = w_gate_ffn.shape[2]
    assert d % LANES == 0 and s % max(ATT_TQ, MIX_TM, MERGE_TM, XA_TM) == 0
    assert ATT_KG == ATT_TQ and ATT_KG % LANES == 0 and SUB_TM % SGU_LEN == 0
    assert all(t % SUB_TM == 0 for t in (MIX_TM, MERGE_TM, XA_TM, FFN_TM))
    assert (b * s) % FFN_TM == 0 and wd == SB_HEADS * SB_HEAD_DIM
    assert hid % FFN_TH == 0 and w_in.shape[2] == 8 * wd + N_BRANCHES * d

    rows3 = lambda v: v.reshape(depth, 1, -1)
    w_in_bf = w_in.astype(BF16)
    w_branch_bf, w_out_bf = w_branch.astype(BF16), w_out.astype(BF16)
    w_q_bf, w_o_bf = w_q_xa.astype(BF16), w_o_xa.astype(BF16)
    w_kv_bf = jnp.concatenate([w_k_xa, w_v_xa], axis=2).astype(BF16)
    w_gate_bf, w_up_bf, w_down_bf = (w.astype(BF16) for w in (w_gate_ffn, w_up_ffn, w_down_ffn))
    g_mix, g_xa, g_mem, g_ffn = (rows3(g) for g in (norm_mix_g, norm_xa_g, mem_norm_g, norm_ffn_g))
    ln_g, ln_b = rows3(sgu_ln_g), rows3(sgu_ln_b)
    b_s_t = jnp.swapaxes(b_spatial, 1, 2)

    xf = x.reshape(b * s, d)
    memf = mem.reshape(b * n_mem, d)
    for l in range(depth):
        qkv, yb, yc = _mix_in(xf, g_mix, w_in_bf, ln_g, ln_b, w_spatial, b_s_t, conv_w, s, l, f"mix_in_{l}")
        ya = _sb_attention(qkv.reshape(b, s, 3 * wd), f"sb_attn_{l}").reshape(b * s, wd)
        xf = _merge(xf, g_mix, w_in_bf, ya, yb, yc, w_branch_bf, w_out_bf, l, f"merge_{l}")
        kv = _norm_matmul(memf, g_mem, w_kv_bf, l, f"mem_kv_{l}").reshape(b, n_mem, 2 * d)
        xf = _xattn(xf.reshape(b, s, d), g_xa, w_q_bf, kv, w_o_bf, l, f"xattn_{l}").reshape(b * s, d)
        xf = _ffn(xf, g_ffn, w_gate_bf, w_up_bf, w_down_bf, final_g.reshape(1, d), l == depth - 1, l, f"ffn_{l}")
    return xf.reshape(b, s, d)
```

```python
import functools

import jax
import jax.numpy as jnp
from jax import lax
from jax.experimental import pallas as pl
from jax.experimental.pallas import tpu as pltpu

F32 = jnp.float32
BF16 = jnp.bfloat16

LANES = 128
V7X_VMEM_LIMIT_BYTES = 56 * 1024 * 1024

SB_HEADS = 8
SB_HEAD_DIM = 64
SGU_LEN = 128
SGU_GROUPS = 4
CHUNK = 64
CONV_WIDTH = 3
N_BRANCHES = 3
XA_HEADS = 4
RMS_EPS = 1e-6
LN_EPS = 1e-5
LOG2E = 1.4426950408889634
MASKED_LOG_WEIGHT = -1e30
DEAD_LOG_WEIGHT = -120.0

PROJ_TM = 1024
PROJ_TN = 512
SUB_TM = 512
MIX_TM = 1024
ATT_TQ = 256
ATT_TM = 512
ATT_KG = 256
MERGE_TM = 1024
XA_TM = 2048
FFN_TM = 1024
FFN_TH = 256
CONV_HALO = 8


def _params(*semantics):
    return pltpu.CompilerParams(dimension_semantics=semantics,
                                vmem_limit_bytes=V7X_VMEM_LIMIT_BYTES)


def _layer_spec(shape, layer):
    zeros = (0,) * len(shape)
    return pl.BlockSpec((None, *shape), lambda *_: (layer, *zeros), pipeline_mode=pl.Buffered(1))


def _sub_tiles(n_rows):
    return [slice(r, r + SUB_TM) for r in range(0, n_rows, SUB_TM)]


def _rms(x, g):
    return x * lax.rsqrt(jnp.mean(x * x, axis=-1, keepdims=True) + RMS_EPS) * g


def _dot(a, b):
    return jnp.dot(a, b, preferred_element_type=F32)


def _dot_nt(a, b):
    return lax.dot_general(a, b, (((1,), (1,)), ((), ())), preferred_element_type=F32)


def _norm_matmul_kernel(x_ref, g_ref, w_ref, o_ref, h_ref):
    @pl.when(pl.program_id(1) == 0)
    def _():
        h_ref[...] = _rms(x_ref[...], g_ref[...]).astype(BF16)

    o_ref[...] = _dot(h_ref[...], w_ref[...]).astype(o_ref.dtype)


def _norm_matmul(x, g, w, layer, name):
    rows, d = x.shape
    cols = w.shape[2]
    tm = min(PROJ_TM, rows)
    return pl.pallas_call(
        _norm_matmul_kernel,
        out_shape=jax.ShapeDtypeStruct((rows, cols), BF16),
        grid=(rows // tm, cols // PROJ_TN),
        in_specs=[
            pl.BlockSpec((tm, d), lambda i, j: (i, 0)),
            _layer_spec((1, d), layer),
            pl.BlockSpec((None, d, PROJ_TN), lambda i, j: (layer, 0, j)),
        ],
        out_specs=pl.BlockSpec((tm, PROJ_TN), lambda i, j: (i, j)),
        scratch_shapes=[pltpu.VMEM((tm, d), BF16)],
        compiler_params=_params("parallel", "arbitrary"),
        name=name,
    )(x, g, w)


def _spatial_gating(z, lng_ref, lnb_ref, ws_ref, bs_ref, o_ref):
    tm = z.shape[0]
    wd = o_ref.shape[1]
    gd = wd // SGU_GROUPS
    z = 0.5 * z * (1.0 + lax.erf(z * (2.0 ** -0.5)))
    u = z[:, :wd]
    v = z[:, wd:]
    mu = jnp.mean(v, axis=-1, keepdims=True)
    vc = v - mu
    v = vc * lax.rsqrt(jnp.mean(vc * vc, axis=-1, keepdims=True) + LN_EPS)
    v = (v * lng_ref[...] + lnb_ref[...]).astype(BF16)
    pos_t = lax.broadcasted_iota(jnp.int32, (SGU_LEN, SGU_LEN), 0)
    pos_s = lax.broadcasted_iota(jnp.int32, (SGU_LEN, SGU_LEN), 1)
    allowed = (pos_t // CHUNK) >= (pos_s // CHUNK)
    for gi in range(SGU_GROUPS):
        wm = jnp.where(allowed, ws_ref[gi], 0.0).astype(BF16)
        bias = bs_ref[:, gi:gi + 1]
        for c in range(tm // SGU_LEN):
            rows = slice(c * SGU_LEN, (c + 1) * SGU_LEN)
            cols = slice(gi * gd, (gi + 1) * gd)
            vm = _dot(wm, v[rows, cols]) + bias
            o_ref[rows, cols] = (u[rows, cols] * vm).astype(o_ref.dtype)


def _gated_conv(p, cw_ref, o_ref, halo):
    tm = p.shape[0]
    wd = o_ref.shape[1]
    cb = p[:, :wd]
    y = p[:, wd:2 * wd] * p[:, 2 * wd:]
    prev1 = halo[CONV_HALO - 1:CONV_HALO, :]
    prev2 = halo[CONV_HALO - 2:CONV_HALO - 1, :]
    row = lax.broadcasted_iota(jnp.int32, (tm, wd), 0)
    y1 = jnp.where(row == 0, prev1, pltpu.roll(y, 1, axis=0))
    y2 = jnp.where(row == 0, prev2, jnp.where(row == 1, prev1, pltpu.roll(y, 2, axis=0)))
    cw = cw_ref[...]
    conv = cw[0:1, :] * y2 + cw[1:2, :] * y1 + cw[2:3, :] * y
    o_ref[...] = (cb * conv).astype(o_ref.dtype)
    return y[tm - CONV_HALO:, :]


def _mix_in_kernel(x_ref, g_ref, w_ref, lng_ref, lnb_ref, ws_ref, bs_ref, cw_ref,
                   qkv_ref, yb_ref, yc_ref, halo_ref, *, tiles_per_seq):
    wd = yb_ref.shape[1]
    c_qkv, c_sgu = 3 * wd, 5 * wd
    halo = jnp.where(pl.program_id(0) % tiles_per_seq == 0, 0.0, halo_ref[...])
    subs = _sub_tiles(x_ref.shape[0])
    proj = []
    for rows in subs:
        h = _rms(x_ref[rows, :], g_ref[...]).astype(BF16)
        qkv_ref[rows, :] = _dot(h, w_ref[:, :c_qkv]).astype(qkv_ref.dtype)
        proj.append((_dot(h, w_ref[:, c_qkv:c_sgu]), _dot(h, w_ref[:, c_sgu:])))
    for rows, (z, p) in zip(subs, proj):
        _spatial_gating(z, lng_ref, lnb_ref, ws_ref, bs_ref, yb_ref.at[rows, :])
        halo = _gated_conv(p, cw_ref, yc_ref.at[rows, :], halo)
    halo_ref[...] = halo


def _mix_in(x, g, w_in, ln_g, ln_b, w_s, b_s_t, conv_w, seq, layer, name):
    rows, d = x.shape
    wd = ln_g.shape[2]
    tm = MIX_TM
    row_w = pl.BlockSpec((tm, wd), lambda i: (i, 0))
    return pl.pallas_call(
        functools.partial(_mix_in_kernel, tiles_per_seq=seq // tm),
        out_shape=(jax.ShapeDtypeStruct((rows, 3 * wd), BF16),
                   jax.ShapeDtypeStruct((rows, wd), BF16),
                   jax.ShapeDtypeStruct((rows, wd), BF16)),
        grid=(rows // tm,),
        in_specs=[
            pl.BlockSpec((tm, d), lambda i: (i, 0)),
            _layer_spec((1, d), layer),
            _layer_spec((d, 8 * wd), layer),
            _layer_spec((1, wd), layer),
            _layer_spec((1, wd), layer),
            _layer_spec((SGU_GROUPS, SGU_LEN, SGU_LEN), layer),
            _layer_spec((SGU_LEN, SGU_GROUPS), layer),
            _layer_spec((CONV_WIDTH, wd), layer),
        ],
        out_specs=(pl.BlockSpec((tm, 3 * wd), lambda i: (i, 0)), row_w, row_w),
        scratch_shapes=[pltpu.VMEM((CONV_HALO, wd), F32)],
        compiler_params=_params("arbitrary"),
        name=name,
    )(x, g, w_in, ln_g, ln_b, w_s, b_s_t, conv_w)


def _sb_attn_kernel(q_ref, k_ref, v_ref, tri_ref, o_ref, acc_ref, carry_ref):
    tq, kg = ATT_TQ, ATT_KG
    n_pairs = q_ref.shape[1] // LANES
    n_qb = q_ref.shape[0] // tq
    first_qb = pl.program_id(1) * n_qb
    lane = lax.broadcasted_iota(jnp.int32, (tq, LANES), 1)
    row = lax.broadcasted_iota(jnp.int32, (2 * tq, kg), 0)
    col = lax.broadcasted_iota(jnp.int32, (2 * tq, kg), 1)
    diag_bias = jnp.where(col < jnp.where(row >= tq, row - tq, row), 0.0, MASKED_LOG_WEIGHT)

    def stacked_queries(qb, p):
        q = q_ref[qb * tq:(qb + 1) * tq, p * LANES:(p + 1) * LANES] * jnp.asarray(SB_HEAD_DIM ** -0.5, BF16)
        zero = jnp.zeros_like(q)
        return jnp.concatenate([jnp.where(lane < SB_HEAD_DIM, q, zero),
                                jnp.where(lane >= SB_HEAD_DIM, q, zero)], axis=0)

    def group(qb, p, g, bias=None):
        keys = pl.ds(pl.multiple_of(g * kg, kg), kg)
        z = _dot_nt(q2s[qb][p], k_ref[keys, p * LANES:(p + 1) * LANES])
        if bias is not None:
            z = z + bias
        sp = jnp.log(1.0 + jnp.exp2(jnp.abs(z) * -LOG2E))
        log_sig = jnp.minimum(z, 0.0) - sp
        log_1m = log_sig - z
        carry = carry_ref[qb, p]
        suffix = _dot(log_1m.astype(BF16), tri_ref[...])
        a = jnp.exp(log_sig + suffix + carry)
        acc_ref[qb, p] += _dot(a.astype(BF16), v_ref[keys, p * LANES:(p + 1) * LANES])
        carry_ref[qb, p] = carry + jnp.sum(log_1m, axis=-1, keepdims=True)

    acc_ref[...] = jnp.zeros_like(acc_ref)
    carry_ref[...] = jnp.zeros_like(carry_ref)
    q2s = [[stacked_queries(qb, p) for p in range(n_pairs)] for qb in range(n_qb)]
    for qb in range(n_qb):
        for p in range(n_pairs):
            group(qb, p, first_qb + qb, diag_bias)
    for qb in range(n_qb):
        i = first_qb + qb
        bias = None if qb > 0 else jnp.where(i >= 1, 0.0, MASKED_LOG_WEIGHT)
        for p in range(n_pairs):
            group(qb, p, jnp.maximum(i - 1, 0), bias)

    for qb in range(n_qb):
        i = first_qb + qb

        @pl.when(jnp.logical_and(i >= 2, jnp.max(carry_ref[qb]) > DEAD_LOG_WEIGHT))
        def _(qb=qb, i=i):
            for p in range(n_pairs):
                def step(g, p=p):
                    group(qb, p, g)
                    return g - 1

                lax.while_loop(
                    lambda g, p=p: jnp.logical_and(g >= 0, jnp.max(carry_ref[qb, p]) > DEAD_LOG_WEIGHT),
                    step, i - 2)

    for qb in range(n_qb):
        for p in range(n_pairs):
            acc = acc_ref[qb, p]
            o_ref[qb * tq:(qb + 1) * tq, p * LANES:(p + 1) * LANES] = jnp.where(
                lane < SB_HEAD_DIM, acc[:tq], acc[tq:]).astype(o_ref.dtype)


def _tri_matrix():
    j = lax.broadcasted_iota(jnp.int32, (ATT_KG, ATT_KG), 0)
    s = lax.broadcasted_iota(jnp.int32, (ATT_KG, ATT_KG), 1)
    return (j > s).astype(BF16)


def _sb_attention(qkv, name):
    b, s, _ = qkv.shape
    tq = ATT_TM
    width = SB_HEADS * SB_HEAD_DIM
    pairs = width // LANES
    return pl.pallas_call(
        _sb_attn_kernel,
        out_shape=jax.ShapeDtypeStruct((b, s, width), BF16),
        grid=(b, s // tq),
        in_specs=[
            pl.BlockSpec((None, tq, width), lambda bi, i: (bi, i, 0)),
            pl.BlockSpec((None, s, width), lambda bi, i: (bi, 0, 1)),
            pl.BlockSpec((None, s, width), lambda bi, i: (bi, 0, 2)),
            pl.BlockSpec((ATT_KG, ATT_KG), lambda bi, i: (0, 0)),
        ],
        out_specs=pl.BlockSpec((None, tq, width), lambda bi, i: (bi, i, 0)),
        scratch_shapes=[
            pltpu.VMEM((tq // ATT_TQ, pairs, 2 * ATT_TQ, LANES), F32),
            pltpu.VMEM((tq // ATT_TQ, pairs, 2 * ATT_TQ, 1), F32),
        ],
        compiler_params=_params("parallel", "arbitrary"),
        name=name,
    )(qkv, qkv, qkv, _tri_matrix())


def _merge_kernel(x_ref, g_ref, wga_ref, wgb_ref, wgc_ref, ya_ref, yb_ref, yc_ref, wb_ref, wo_ref, o_ref):
    for rows in _sub_tiles(x_ref.shape[0]):
        x = x_ref[rows, :]
        h = _rms(x, g_ref[...]).astype(BF16)
        merged = None
        for n, (wg_ref, y_ref) in enumerate(((wga_ref, ya_ref), (wgb_ref, yb_ref), (wgc_ref, yc_ref))):
            gate = jax.nn.sigmoid(_dot(h, wg_ref[...]))
            term = gate * _dot(y_ref[rows, :], wb_ref[n])
            merged = term if merged is None else merged + term
        o_ref[rows, :] = x + _dot(merged.astype(BF16), wo_ref[...])


def _merge(x, g, w_in, ya, yb, yc, w_branch, w_out, layer, name):
    rows, d = x.shape
    wd = ya.shape[1]
    tm = MERGE_TM
    first_gate_block = (w_in.shape[2] - N_BRANCHES * d) // d
    row_d = pl.BlockSpec((tm, d), lambda i: (i, 0))
    row_w = pl.BlockSpec((tm, wd), lambda i: (i, 0))
    gate_w = [pl.BlockSpec((None, d, d), lambda i, n=n: (layer, 0, first_gate_block + n),
                           pipeline_mode=pl.Buffered(1)) for n in range(N_BRANCHES)]
    return pl.pallas_call(
        _merge_kernel,
        out_shape=jax.ShapeDtypeStruct((rows, d), F32),
        grid=(rows // tm,),
        in_specs=[
            row_d,
            _layer_spec((1, d), layer),
            *gate_w,
            row_w, row_w, row_w,
            _layer_spec((N_BRANCHES, wd, d), layer),
            _layer_spec((d, d), layer),
        ],
        out_specs=row_d,
        compiler_params=_params("parallel"),
        name=name,
    )(x, g, w_in, w_in, w_in, ya, yb, yc, w_branch, w_out)


def _xattn_kernel(x_ref, g_ref, wq_ref, k_ref, v_ref, wo_ref, o_ref):
    d = x_ref.shape[1]
    dh = d // XA_HEADS
    for rows in _sub_tiles(x_ref.shape[0]):
        x = x_ref[rows, :]
        h = _rms(x, g_ref[...]).astype(BF16)
        q = (_dot(h, wq_ref[...]) * (dh ** -0.5)).astype(BF16)
        heads = []
        for hd in range(XA_HEADS):
            cols = slice(hd * dh, (hd + 1) * dh)
            s = _dot_nt(q[:, cols], k_ref[:, cols])
            p = jnp.exp(s - jnp.max(s, axis=-1, keepdims=True))
            p = p / jnp.sum(p, axis=-1, keepdims=True)
            heads.append(_dot(p.astype(BF16), v_ref[:, cols]).astype(BF16))
        o_ref[rows, :] = x + _dot(jnp.concatenate(heads, axis=1), wo_ref[...])


def _xattn(x, g, wq, kv, wo, layer, name):
    b, s, d = x.shape
    mem = kv.shape[1]
    tm = XA_TM
    row_d = pl.BlockSpec((None, tm, d), lambda bi, i: (bi, i, 0))
    return pl.pallas_call(
        _xattn_kernel,
        out_shape=jax.ShapeDtypeStruct((b, s, d), F32),
        grid=(b, s // tm),
        in_specs=[
            row_d,
            _layer_spec((1, d), layer),
            _layer_spec((d, d), layer),
            pl.BlockSpec((None, mem, d), lambda bi, i: (bi, 0, 0)),
            pl.BlockSpec((None, mem, d), lambda bi, i: (bi, 0, 1)),
            _layer_spec((d, d), layer),
        ],
        out_specs=row_d,
        compiler_params=_params("parallel", "parallel"),
        name=name,
    )(x, g, wq, kv, kv, wo)


def _ffn_kernel(x_ref, g_ref, wg_ref, wu_ref, wd_ref, fg_ref, o_ref, h_ref, acc_ref, *, final_norm):
    subs = _sub_tiles(x_ref.shape[0])
    for rows in subs:
        h_ref[rows, :] = _rms(x_ref[rows, :], g_ref[...]).astype(BF16)
    for c in range(0, wg_ref.shape[1], FFN_TH):
        cols = slice(c, c + FFN_TH)
        for rows in subs:
            h = h_ref[rows, :]
            a = _dot(h, wg_ref[:, cols])
            t = ((a * jax.nn.sigmoid(a)) * _dot(h, wu_ref[:, cols])).astype(BF16)
            if c == 0:
                acc_ref[rows, :] = _dot(t, wd_ref[cols, :])
            else:
                acc_ref[rows, :] += _dot(t, wd_ref[cols, :])
    for rows in subs:
        y = x_ref[rows, :] + acc_ref[rows, :]
        if final_norm:
            y = _rms(y, fg_ref[...])
        o_ref[rows, :] = y


def _ffn(x, g, w_gate, w_up, w_down, final_g, final_norm, layer, name):
    rows, d = x.shape
    hid = w_gate.shape[2]
    tm = FFN_TM
    row_d = pl.BlockSpec((tm, d), lambda i: (i, 0))
    return pl.pallas_call(
        functools.partial(_ffn_kernel, final_norm=final_norm),
        out_shape=jax.ShapeDtypeStruct((rows, d), F32),
        grid=(rows // tm,),
        in_specs=[
            row_d,
            _layer_spec((1, d), layer),
            _layer_spec((d, hid), layer),
            _layer_spec((d, hid), layer),
            _layer_spec((hid, d), layer),
            pl.BlockSpec((1, d), lambda i: (0, 0)),
        ],
        out_specs=row_d,
        scratch_shapes=[pltpu.VMEM((tm, d), BF16), pltpu.VMEM((tm, d), F32)],
        compiler_params=_params("parallel"),
        name=name,
    )(x, g, w_gate, w_up, w_down, final_g)


def kernel(x, mem, norm_mix_g, w_in, sgu_ln_g, sgu_ln_b, w_spatial, b_spatial, conv_w, w_branch, w_out, norm_xa_g, mem_norm_g, w_q_xa, w_k_xa, w_v_xa, w_o_xa, norm_ffn_g, w_gate_ffn, w_up_ffn, w_down_ffn, final_g):
    b, s, d = x.shape
    depth = w_in.shape[0]
    wd = w_branch.shape[2]
    n_mem = mem.shape[1]
    hid = w_gate_ffn.shape[2]
    assert d % LANES == 0 and s % max(ATT_TM, MIX_TM, MERGE_TM, XA_TM) == 0
    assert ATT_KG == ATT_TQ and ATT_KG % LANES == 0 and ATT_TM % ATT_TQ == 0 and SUB_TM % SGU_LEN == 0
    assert all(t % SUB_TM == 0 for t in (MIX_TM, MERGE_TM, XA_TM, FFN_TM))
    assert (b * s) % FFN_TM == 0 and wd == SB_HEADS * SB_HEAD_DIM
    assert hid % FFN_TH == 0 and w_in.shape[2] == 8 * wd + N_BRANCHES * d

    rows3 = lambda v: v.reshape(depth, 1, -1)
    w_in_bf = w_in.astype(BF16)
    w_branch_bf, w_out_bf = w_branch.astype(BF16), w_out.astype(BF16)
    w_q_bf, w_o_bf = w_q_xa.astype(BF16), w_o_xa.astype(BF16)
    w_kv_bf = jnp.concatenate([w_k_xa, w_v_xa], axis=2).astype(BF16)
    w_gate_bf, w_up_bf, w_down_bf = (w.astype(BF16) for w in (w_gate_ffn, w_up_ffn, w_down_ffn))
    g_mix, g_xa, g_mem, g_ffn = (rows3(g) for g in (norm_mix_g, norm_xa_g, mem_norm_g, norm_ffn_g))
    ln_g, ln_b = rows3(sgu_ln_g), rows3(sgu_ln_b)
    b_s_t = jnp.swapaxes(b_spatial, 1, 2)

    xf = x.reshape(b * s, d)
    memf = mem.reshape(b * n_mem, d)
    for l in range(depth):
        qkv, yb, yc = _mix_in(xf, g_mix, w_in_bf, ln_g, ln_b, w_spatial, b_s_t, conv_w, s, l, f"mix_in_{l}")
        ya = _sb_attention(qkv.reshape(b, s, 3 * wd), f"sb_attn_{l}").reshape(b * s, wd)
        xf = _merge(xf, g_mix, w_in_bf, ya, yb, yc, w_branch_bf, w_out_bf, l, f"merge_{l}")
        kv = _norm_matmul(memf, g_mem, w_kv_bf, l, f"mem_kv_{l}").reshape(b, n_mem, 2 * d)
        xf = _xattn(xf.reshape(b, s, d), g_xa, w_q_bf, kv, w_o_bf, l, f"xattn_{l}").reshape(b * s, d)
        xf = _ffn(xf, g_ffn, w_gate_bf, w_up_bf, w_down_bf, final_g.reshape(1, d), l == depth - 1, l, f"ffn_{l}")
    return xf.reshape(b, s, d)
```

```python
import functools

import jax
import jax.numpy as jnp
from jax import lax
from jax.experimental import pallas as pl
from jax.experimental.pallas import tpu as pltpu

F32 = jnp.float32
BF16 = jnp.bfloat16

LANES = 128
V7X_VMEM_LIMIT_BYTES = 56 * 1024 * 1024

SB_HEADS = 8
SB_HEAD_DIM = 64
SGU_LEN = 128
SGU_GROUPS = 4
CHUNK = 64
CONV_WIDTH = 3
N_BRANCHES = 3
XA_HEADS = 4
RMS_EPS = 1e-6
LN_EPS = 1e-5
LOG2E = 1.4426950408889634
MASKED_LOG_WEIGHT = -1e30
DEAD_LOG_WEIGHT = -120.0

PROJ_TM = 1024
PROJ_TN = 512
SUB_TM = 512
MIX_TM = 1024
ATT_TQ = 256
ATT_TM = 512
ATT_KG = 256
MERGE_TM = 1024
XA_TM = 2048
FFN_TM = 1024
FFN_TH = 256
CONV_HALO = 8


def _params(*semantics):
    return pltpu.CompilerParams(dimension_semantics=semantics,
                                vmem_limit_bytes=V7X_VMEM_LIMIT_BYTES)


def _layer_spec(shape, layer):
    zeros = (0,) * len(shape)
    return pl.BlockSpec((None, *shape), lambda *_: (layer, *zeros), pipeline_mode=pl.Buffered(1))


def _sub_tiles(n_rows):
    return [slice(r, r + SUB_TM) for r in range(0, n_rows, SUB_TM)]


def _rms(x, g):
    return x * lax.rsqrt(jnp.mean(x * x, axis=-1, keepdims=True) + RMS_EPS) * g


def _dot(a, b):
    return jnp.dot(a, b, preferred_element_type=F32)


def _dot_nt(a, b):
    return lax.dot_general(a, b, (((1,), (1,)), ((), ())), preferred_element_type=F32)


def _mem_kv_kernel(x_ref, g_ref, wk_ref, wv_ref, k_ref, v_ref, h_ref):
    @pl.when(pl.program_id(1) == 0)
    def _():
        h_ref[...] = _rms(x_ref[...], g_ref[...]).astype(BF16)

    h = h_ref[...]
    k_ref[...] = _dot(h, wk_ref[...].astype(BF16)).astype(k_ref.dtype)
    v_ref[...] = _dot(h, wv_ref[...].astype(BF16)).astype(v_ref.dtype)


def _mem_kv(x, g, wk, wv, layer, name):
    rows, d = x.shape
    cols = wk.shape[2]
    tm = min(PROJ_TM, rows)
    w_spec = pl.BlockSpec((None, d, PROJ_TN), lambda i, j: (layer, 0, j))
    o_spec = pl.BlockSpec((tm, PROJ_TN), lambda i, j: (i, j))
    return pl.pallas_call(
        _mem_kv_kernel,
        out_shape=(jax.ShapeDtypeStruct((rows, cols), BF16), jax.ShapeDtypeStruct((rows, cols), BF16)),
        grid=(rows // tm, cols // PROJ_TN),
        in_specs=[
            pl.BlockSpec((tm, d), lambda i, j: (i, 0)),
            _layer_spec((1, d), layer),
            w_spec, w_spec,
        ],
        out_specs=(o_spec, o_spec),
        scratch_shapes=[pltpu.VMEM((tm, d), BF16)],
        compiler_params=_params("parallel", "arbitrary"),
        name=name,
    )(x, g, wk, wv)


def _spatial_gating(z, lng_ref, lnb_ref, ws_ref, bs_ref, o_ref):
    tm = z.shape[0]
    wd = o_ref.shape[1]
    gd = wd // SGU_GROUPS
    z = 0.5 * z * (1.0 + lax.erf(z * (2.0 ** -0.5)))
    u = z[:, :wd]
    v = z[:, wd:]
    mu = jnp.mean(v, axis=-1, keepdims=True)
    vc = v - mu
    v = vc * lax.rsqrt(jnp.mean(vc * vc, axis=-1, keepdims=True) + LN_EPS)
    v = (v * lng_ref[...] + lnb_ref[...]).astype(BF16)
    pos_t = lax.broadcasted_iota(jnp.int32, (SGU_LEN, SGU_LEN), 0)
    pos_s = lax.broadcasted_iota(jnp.int32, (SGU_LEN, SGU_LEN), 1)
    allowed = (pos_t // CHUNK) >= (pos_s // CHUNK)
    for gi in range(SGU_GROUPS):
        wm = jnp.where(allowed, ws_ref[gi], 0.0).astype(BF16)
        bias = bs_ref[:, gi:gi + 1]
        for c in range(tm // SGU_LEN):
            rows = slice(c * SGU_LEN, (c + 1) * SGU_LEN)
            cols = slice(gi * gd, (gi + 1) * gd)
            vm = _dot(wm, v[rows, cols]) + bias
            o_ref[rows, cols] = (u[rows, cols] * vm).astype(o_ref.dtype)


def _gated_conv(p, cw_ref, o_ref, halo):
    tm = p.shape[0]
    wd = o_ref.shape[1]
    cb = p[:, :wd]
    y = p[:, wd:2 * wd] * p[:, 2 * wd:]
    prev1 = halo[CONV_HALO - 1:CONV_HALO, :]
    prev2 = halo[CONV_HALO - 2:CONV_HALO - 1, :]
    row = lax.broadcasted_iota(jnp.int32, (tm, wd), 0)
    y1 = jnp.where(row == 0, prev1, pltpu.roll(y, 1, axis=0))
    y2 = jnp.where(row == 0, prev2, jnp.where(row == 1, prev1, pltpu.roll(y, 2, axis=0)))
    cw = cw_ref[...]
    conv = cw[0:1, :] * y2 + cw[1:2, :] * y1 + cw[2:3, :] * y
    o_ref[...] = (cb * conv).astype(o_ref.dtype)
    return y[tm - CONV_HALO:, :]


def _mix_in_kernel(x_ref, g_ref, w_ref, lng_ref, lnb_ref, ws_ref, bs_ref, cw_ref,
                   qkv_ref, yb_ref, yc_ref, halo_ref, *, tiles_per_seq):
    wd = yb_ref.shape[1]
    c_qkv, c_sgu = 3 * wd, 5 * wd
    halo = jnp.where(pl.program_id(0) % tiles_per_seq == 0, 0.0, halo_ref[...])
    subs = _sub_tiles(x_ref.shape[0])
    proj = []
    for rows in subs:
        h = _rms(x_ref[rows, :], g_ref[...]).astype(BF16)
        qkv_ref[rows, :] = _dot(h, w_ref[:, :c_qkv]).astype(qkv_ref.dtype)
        proj.append((_dot(h, w_ref[:, c_qkv:c_sgu]), _dot(h, w_ref[:, c_sgu:])))
    for rows, (z, p) in zip(subs, proj):
        _spatial_gating(z, lng_ref, lnb_ref, ws_ref, bs_ref, yb_ref.at[rows, :])
        halo = _gated_conv(p, cw_ref, yc_ref.at[rows, :], halo)
    halo_ref[...] = halo


def _mix_in(x, g, w_in, ln_g, ln_b, w_s, b_s_t, conv_w, seq, layer, name):
    rows, d = x.shape
    wd = ln_g.shape[2]
    tm = MIX_TM
    row_w = pl.BlockSpec((tm, wd), lambda i: (i, 0))
    return pl.pallas_call(
        functools.partial(_mix_in_kernel, tiles_per_seq=seq // tm),
        out_shape=(jax.ShapeDtypeStruct((rows, 3 * wd), BF16),
                   jax.ShapeDtypeStruct((rows, wd), BF16),
                   jax.ShapeDtypeStruct((rows, wd), BF16)),
        grid=(rows // tm,),
        in_specs=[
            pl.BlockSpec((tm, d), lambda i: (i, 0)),
            _layer_spec((1, d), layer),
            _layer_spec((d, 8 * wd), layer),
            _layer_spec((1, wd), layer),
            _layer_spec((1, wd), layer),
            _layer_spec((SGU_GROUPS, SGU_LEN, SGU_LEN), layer),
            _layer_spec((SGU_LEN, SGU_GROUPS), layer),
            _layer_spec((CONV_WIDTH, wd), layer),
        ],
        out_specs=(pl.BlockSpec((tm, 3 * wd), lambda i: (i, 0)), row_w, row_w),
        scratch_shapes=[pltpu.VMEM((CONV_HALO, wd), F32)],
        compiler_params=_params("arbitrary"),
        name=name,
    )(x, g, w_in, ln_g, ln_b, w_s, b_s_t, conv_w)


def _sb_attn_kernel(q_ref, k_ref, v_ref, tri_ref, o_ref, acc_ref, carry_ref):
    tq, kg = ATT_TQ, ATT_KG
    n_pairs = q_ref.shape[1] // LANES
    n_qb = q_ref.shape[0] // tq
    first_qb = pl.program_id(1) * n_qb
    lane = lax.broadcasted_iota(jnp.int32, (tq, LANES), 1)
    row = lax.broadcasted_iota(jnp.int32, (2 * tq, kg), 0)
    col = lax.broadcasted_iota(jnp.int32, (2 * tq, kg), 1)
    diag_bias = jnp.where(col < jnp.where(row >= tq, row - tq, row), 0.0, MASKED_LOG_WEIGHT)

    def stacked_queries(qb, p):
        q = q_ref[qb * tq:(qb + 1) * tq, p * LANES:(p + 1) * LANES] * jnp.asarray(SB_HEAD_DIM ** -0.5, BF16)
        zero = jnp.zeros_like(q)
        return jnp.concatenate([jnp.where(lane < SB_HEAD_DIM, q, zero),
                                jnp.where(lane >= SB_HEAD_DIM, q, zero)], axis=0)

    def group(qb, p, g, bias=None, first=False):
        keys = pl.ds(pl.multiple_of(g * kg, kg), kg)
        z = _dot_nt(q2s[qb][p], k_ref[keys, p * LANES:(p + 1) * LANES])
        if bias is not None:
            z = z + bias
        sp = jnp.log(1.0 + jnp.exp2(jnp.abs(z) * -LOG2E))
        log_sig = jnp.minimum(z, 0.0) - sp
        log_1m = log_sig - z
        suffix = _dot(log_1m.astype(BF16), tri_ref[...])
        total = jnp.sum(log_1m, axis=-1, keepdims=True)
        vs = v_ref[keys, p * LANES:(p + 1) * LANES]
        if first:
            acc_ref[qb, p] = _dot(jnp.exp(log_sig + suffix).astype(BF16), vs)
            carry_ref[qb, p] = total
        else:
            carry = carry_ref[qb, p]
            acc_ref[qb, p] += _dot(jnp.exp(log_sig + suffix + carry).astype(BF16), vs)
            carry_ref[qb, p] = carry + total

    q2s = [[stacked_queries(qb, p) for p in range(n_pairs)] for qb in range(n_qb)]
    for qb in range(n_qb):
        for p in range(n_pairs):
            group(qb, p, first_qb + qb, diag_bias, first=True)
    for qb in range(n_qb):
        i = first_qb + qb
        bias = None if qb > 0 else jnp.where(i >= 1, 0.0, MASKED_LOG_WEIGHT)
        for p in range(n_pairs):
            group(qb, p, jnp.maximum(i - 1, 0), bias)

    for qb in range(n_qb):
        i = first_qb + qb

        @pl.when(jnp.logical_and(i >= 2, jnp.max(carry_ref[qb]) > DEAD_LOG_WEIGHT))
        def _(qb=qb, i=i):
            for p in range(n_pairs):
                def step(g, p=p):
                    group(qb, p, g)
                    return g - 1

                lax.while_loop(
                    lambda g, p=p: jnp.logical_and(g >= 0, jnp.max(carry_ref[qb, p]) > DEAD_LOG_WEIGHT),
                    step, i - 2)

    for qb in range(n_qb):
        for p in range(n_pairs):
            acc = acc_ref[qb, p]
            o_ref[qb * tq:(qb + 1) * tq, p * LANES:(p + 1) * LANES] = jnp.where(
                lane < SB_HEAD_DIM, acc[:tq], acc[tq:]).astype(o_ref.dtype)


def _tri_matrix():
    j = lax.broadcasted_iota(jnp.int32, (ATT_KG, ATT_KG), 0)
    s = lax.broadcasted_iota(jnp.int32, (ATT_KG, ATT_KG), 1)
    return (j > s).astype(BF16)


def _sb_attention(qkv, name):
    b, s, _ = qkv.shape
    tq = ATT_TM
    width = SB_HEADS * SB_HEAD_DIM
    pairs = width // LANES
    return pl.pallas_call(
        _sb_attn_kernel,
        out_shape=jax.ShapeDtypeStruct((b, s, width), BF16),
        grid=(b, s // tq),
        in_specs=[
            pl.BlockSpec((None, tq, width), lambda bi, i: (bi, i, 0)),
            pl.BlockSpec((None, s, width), lambda bi, i: (bi, 0, 1)),
            pl.BlockSpec((None, s, width), lambda bi, i: (bi, 0, 2)),
            pl.BlockSpec((ATT_KG, ATT_KG), lambda bi, i: (0, 0)),
        ],
        out_specs=pl.BlockSpec((None, tq, width), lambda bi, i: (bi, i, 0)),
        scratch_shapes=[
            pltpu.VMEM((tq // ATT_TQ, pairs, 2 * ATT_TQ, LANES), F32),
            pltpu.VMEM((tq // ATT_TQ, pairs, 2 * ATT_TQ, 1), F32),
        ],
        compiler_params=_params("parallel", "arbitrary"),
        name=name,
    )(qkv, qkv, qkv, _tri_matrix())


def _merge_kernel(x_ref, g_ref, wga_ref, wgb_ref, wgc_ref, ya_ref, yb_ref, yc_ref, wb_ref, wo_ref, o_ref):
    for rows in _sub_tiles(x_ref.shape[0]):
        x = x_ref[rows, :]
        h = _rms(x, g_ref[...]).astype(BF16)
        merged = None
        for n, (wg_ref, y_ref) in enumerate(((wga_ref, ya_ref), (wgb_ref, yb_ref), (wgc_ref, yc_ref))):
            gate = jax.nn.sigmoid(_dot(h, wg_ref[...]))
            term = gate * _dot(y_ref[rows, :], wb_ref[n])
            merged = term if merged is None else merged + term
        o_ref[rows, :] = x + _dot(merged.astype(BF16), wo_ref[...])


def _merge(x, g, w_in, ya, yb, yc, w_branch, w_out, layer, name):
    rows, d = x.shape
    wd = ya.shape[1]
    tm = MERGE_TM
    first_gate_block = (w_in.shape[2] - N_BRANCHES * d) // d
    row_d = pl.BlockSpec((tm, d), lambda i: (i, 0))
    row_w = pl.BlockSpec((tm, wd), lambda i: (i, 0))
    gate_w = [pl.BlockSpec((None, d, d), lambda i, n=n: (layer, 0, first_gate_block + n),
                           pipeline_mode=pl.Buffered(1)) for n in range(N_BRANCHES)]
    return pl.pallas_call(
        _merge_kernel,
        out_shape=jax.ShapeDtypeStruct((rows, d), F32),
        grid=(rows // tm,),
        in_specs=[
            row_d,
            _layer_spec((1, d), layer),
            *gate_w,
            row_w, row_w, row_w,
            _layer_spec((N_BRANCHES, wd, d), layer),
            _layer_spec((d, d), layer),
        ],
        out_specs=row_d,
        compiler_params=_params("parallel"),
        name=name,
    )(x, g, w_in, w_in, w_in, ya, yb, yc, w_branch, w_out)


def _xattn_kernel(x_ref, g_ref, wq_ref, k_ref, v_ref, wo_ref, o_ref):
    d = x_ref.shape[1]
    dh = d // XA_HEADS
    wq, wo = wq_ref[...].astype(BF16), wo_ref[...].astype(BF16)
    for rows in _sub_tiles(x_ref.shape[0]):
        x = x_ref[rows, :]
        h = _rms(x, g_ref[...]).astype(BF16)
        q = (_dot(h, wq) * (dh ** -0.5)).astype(BF16)
        heads = []
        for hd in range(XA_HEADS):
            cols = slice(hd * dh, (hd + 1) * dh)
            s = _dot_nt(q[:, cols], k_ref[:, cols])
            p = jnp.exp(s - jnp.max(s, axis=-1, keepdims=True))
            p = p / jnp.sum(p, axis=-1, keepdims=True)
            heads.append(_dot(p.astype(BF16), v_ref[:, cols]).astype(BF16))
        o_ref[rows, :] = x + _dot(jnp.concatenate(heads, axis=1), wo)


def _xattn(x, g, wq, k, v, wo, layer, name):
    b, s, d = x.shape
    mem = k.shape[1]
    tm = XA_TM
    row_d = pl.BlockSpec((None, tm, d), lambda bi, i: (bi, i, 0))
    return pl.pallas_call(
        _xattn_kernel,
        out_shape=jax.ShapeDtypeStruct((b, s, d), F32),
        grid=(b, s // tm),
        in_specs=[
            row_d,
            _layer_spec((1, d), layer),
            _layer_spec((d, d), layer),
            pl.BlockSpec((None, mem, d), lambda bi, i: (bi, 0, 0)),
            pl.BlockSpec((None, mem, d), lambda bi, i: (bi, 0, 0)),
            _layer_spec((d, d), layer),
        ],
        out_specs=row_d,
        compiler_params=_params("parallel", "parallel"),
        name=name,
    )(x, g, wq, k, v, wo)


def _ffn_kernel(x_ref, g_ref, wg_ref, wu_ref, wd_ref, fg_ref, o_ref, h_ref, acc_ref, *, final_norm):
    subs = _sub_tiles(x_ref.shape[0])
    for rows in subs:
        h_ref[rows, :] = _rms(x_ref[rows, :], g_ref[...]).astype(BF16)
    for c in range(0, wg_ref.shape[1], FFN_TH):
        cols = slice(c, c + FFN_TH)
        for rows in subs:
            h = h_ref[rows, :]
            a = _dot(h, wg_ref[:, cols])
            t = ((a * jax.nn.sigmoid(a)) * _dot(h, wu_ref[:, cols])).astype(BF16)
            if c == 0:
                acc_ref[rows, :] = _dot(t, wd_ref[cols, :])
            else:
                acc_ref[rows, :] += _dot(t, wd_ref[cols, :])
    for rows in subs:
        y = x_ref[rows, :] + acc_ref[rows, :]
        if final_norm:
            y = _rms(y, fg_ref[...])
        o_ref[rows, :] = y


def _ffn(x, g, w_gate, w_up, w_down, final_g, final_norm, layer, name):
    rows, d = x.shape
    hid = w_gate.shape[2]
    tm = FFN_TM
    row_d = pl.BlockSpec((tm, d), lambda i: (i, 0))
    return pl.pallas_call(
        functools.partial(_ffn_kernel, final_norm=final_norm),
        out_shape=jax.ShapeDtypeStruct((rows, d), F32),
        grid=(rows // tm,),
        in_specs=[
            row_d,
            _layer_spec((1, d), layer),
            _layer_spec((d, hid), layer),
            _layer_spec((d, hid), layer),
            _layer_spec((hid, d), layer),
            pl.BlockSpec((1, d), lambda i: (0, 0)),
        ],
        out_specs=row_d,
        scratch_shapes=[pltpu.VMEM((tm, d), BF16), pltpu.VMEM((tm, d), F32)],
        compiler_params=_params("parallel"),
        name=name,
    )(x, g, w_gate, w_up, w_down, final_g)


def kernel(x, mem, norm_mix_g, w_in, sgu_ln_g, sgu_ln_b, w_spatial, b_spatial, conv_w, w_branch, w_out, norm_xa_g, mem_norm_g, w_q_xa, w_k_xa, w_v_xa, w_o_xa, norm_ffn_g, w_gate_ffn, w_up_ffn, w_down_ffn, final_g):
    b, s, d = x.shape
    depth = w_in.shape[0]
    wd = w_branch.shape[2]
    n_mem = mem.shape[1]
    hid = w_gate_ffn.shape[2]
    assert d % LANES == 0 and s % max(ATT_TM, MIX_TM, MERGE_TM, XA_TM) == 0
    assert ATT_KG == ATT_TQ and ATT_KG % LANES == 0 and ATT_TM % ATT_TQ == 0 and SUB_TM % SGU_LEN == 0
    assert all(t % SUB_TM == 0 for t in (MIX_TM, MERGE_TM, XA_TM, FFN_TM))
    assert (b * s) % FFN_TM == 0 and wd == SB_HEADS * SB_HEAD_DIM
    assert hid % FFN_TH == 0 and w_in.shape[2] == 8 * wd + N_BRANCHES * d

    rows3 = lambda v: v.reshape(depth, 1, -1)
    w_in_bf = w_in.astype(BF16)
    w_branch_bf, w_out_bf = w_branch.astype(BF16), w_out.astype(BF16)
    w_gate_bf, w_up_bf, w_down_bf = (w.astype(BF16) for w in (w_gate_ffn, w_up_ffn, w_down_ffn))
    g_mix, g_xa, g_mem, g_ffn = (rows3(g) for g in (norm_mix_g, norm_xa_g, mem_norm_g, norm_ffn_g))
    ln_g, ln_b = rows3(sgu_ln_g), rows3(sgu_ln_b)
    b_s_t = jnp.swapaxes(b_spatial, 1, 2)

    xf = x.reshape(b * s, d)
    memf = mem.reshape(b * n_mem, d)
    for l in range(depth):
        qkv, yb, yc = _mix_in(xf, g_mix, w_in_bf, ln_g, ln_b, w_spatial, b_s_t, conv_w, s, l, f"mix_in_{l}")
        ya = _sb_attention(qkv.reshape(b, s, 3 * wd), f"sb_attn_{l}").reshape(b * s, wd)
        xf = _merge(xf, g_mix, w_in_bf, ya, yb, yc, w_branch_bf, w_out_bf, l, f"merge_{l}")
        k_mem, v_mem = (t.reshape(b, n_mem, d) for t in _mem_kv(memf, g_mem, w_k_xa, w_v_xa, l, f"mem_kv_{l}"))
        xf = _xattn(xf.reshape(b, s, d), g_xa, w_q_xa, k_mem, v_mem, w_o_xa, l, f"xattn_{l}").reshape(b * s, d)
        xf = _ffn(xf, g_ffn, w_gate_bf, w_up_bf, w_down_bf, final_g.reshape(1, d), l == depth - 1, l, f"ffn_{l}")
    return xf.reshape(b, s, d)
```

```python
import functools

import jax
import jax.numpy as jnp
from jax import lax
from jax.experimental import pallas as pl
from jax.experimental.pallas import tpu as pltpu

F32 = jnp.float32
BF16 = jnp.bfloat16

LANES = 128
V7X_VMEM_LIMIT_BYTES = 56 * 1024 * 1024

SB_HEADS = 8
SB_HEAD_DIM = 64
SGU_LEN = 128
SGU_GROUPS = 4
CHUNK = 64
CONV_WIDTH = 3
N_BRANCHES = 3
XA_HEADS = 4
RMS_EPS = 1e-6
LN_EPS = 1e-5
LOG2E = 1.4426950408889634
MASKED_LOG_WEIGHT = -1e30
DEAD_LOG_WEIGHT = -120.0

PROJ_TM = 1024
PROJ_TN = 512
SUB_TM = 512
MIX_TM = 1024
ATT_TQ = 256
ATT_TM = 512
ATT_KG = 256
MERGE_TM = 1024
XA_TM = 2048
FFN_TM = 1024
FFN_TH = 256
CONV_HALO = 8


def _params(*semantics):
    return pltpu.CompilerParams(dimension_semantics=semantics,
                                vmem_limit_bytes=V7X_VMEM_LIMIT_BYTES)


def _layer_spec(shape, layer):
    zeros = (0,) * len(shape)
    return pl.BlockSpec((None, *shape), lambda *_: (layer, *zeros), pipeline_mode=pl.Buffered(1))


def _sub_tiles(n_rows):
    return [slice(r, r + SUB_TM) for r in range(0, n_rows, SUB_TM)]


def _rms(x, g):
    return x * lax.rsqrt(jnp.mean(x * x, axis=-1, keepdims=True) + RMS_EPS) * g


def _dot(a, b):
    return jnp.dot(a, b, preferred_element_type=F32)


def _dot_nt(a, b):
    return lax.dot_general(a, b, (((1,), (1,)), ((), ())), preferred_element_type=F32)


def _mem_kv_kernel(x_ref, g_ref, wk_ref, wv_ref, k_ref, v_ref, h_ref):
    @pl.when(pl.program_id(1) == 0)
    def _():
        h_ref[...] = _rms(x_ref[...], g_ref[...]).astype(BF16)

    h = h_ref[...]
    k_ref[...] = _dot(h, wk_ref[...].astype(BF16)).astype(k_ref.dtype)
    v_ref[...] = _dot(h, wv_ref[...].astype(BF16)).astype(v_ref.dtype)


def _mem_kv(x, g, wk, wv, layer, name):
    rows, d = x.shape
    cols = wk.shape[2]
    tm = min(PROJ_TM, rows)
    w_spec = pl.BlockSpec((None, d, PROJ_TN), lambda i, j: (layer, 0, j))
    o_spec = pl.BlockSpec((tm, PROJ_TN), lambda i, j: (i, j))
    return pl.pallas_call(
        _mem_kv_kernel,
        out_shape=(jax.ShapeDtypeStruct((rows, cols), BF16), jax.ShapeDtypeStruct((rows, cols), BF16)),
        grid=(rows // tm, cols // PROJ_TN),
        in_specs=[
            pl.BlockSpec((tm, d), lambda i, j: (i, 0)),
            _layer_spec((1, d), layer),
            w_spec, w_spec,
        ],
        out_specs=(o_spec, o_spec),
        scratch_shapes=[pltpu.VMEM((tm, d), BF16)],
        compiler_params=_params("parallel", "arbitrary"),
        name=name,
    )(x, g, wk, wv)


def _spatial_gating(z, lng_ref, lnb_ref, ws_ref, bs_ref, o_ref):
    tm = z.shape[0]
    wd = o_ref.shape[1]
    gd = wd // SGU_GROUPS
    z = 0.5 * z * (1.0 + lax.erf(z * (2.0 ** -0.5)))
    u = z[:, :wd]
    v = z[:, wd:]
    mu = jnp.mean(v, axis=-1, keepdims=True)
    vc = v - mu
    v = vc * lax.rsqrt(jnp.mean(vc * vc, axis=-1, keepdims=True) + LN_EPS)
    v = (v * lng_ref[...] + lnb_ref[...]).astype(BF16)
    pos_t = lax.broadcasted_iota(jnp.int32, (SGU_LEN, SGU_LEN), 0)
    pos_s = lax.broadcasted_iota(jnp.int32, (SGU_LEN, SGU_LEN), 1)
    allowed = (pos_t // CHUNK) >= (pos_s // CHUNK)
    for gi in range(SGU_GROUPS):
        wm = jnp.where(allowed, ws_ref[gi], 0.0).astype(BF16)
        bias = bs_ref[:, gi:gi + 1]
        for c in range(tm // SGU_LEN):
            rows = slice(c * SGU_LEN, (c + 1) * SGU_LEN)
            cols = slice(gi * gd, (gi + 1) * gd)
            vm = _dot(wm, v[rows, cols]) + bias
            o_ref[rows, cols] = (u[rows, cols] * vm).astype(o_ref.dtype)


def _gated_conv(p, cw_ref, o_ref, halo):
    tm = p.shape[0]
    wd = o_ref.shape[1]
    cb = p[:, :wd]
    y = p[:, wd:2 * wd] * p[:, 2 * wd:]
    prev1 = halo[CONV_HALO - 1:CONV_HALO, :]
    prev2 = halo[CONV_HALO - 2:CONV_HALO - 1, :]
    row = lax.broadcasted_iota(jnp.int32, (tm, wd), 0)
    y1 = jnp.where(row == 0, prev1, pltpu.roll(y, 1, axis=0))
    y2 = jnp.where(row == 0, prev2, jnp.where(row == 1, prev1, pltpu.roll(y, 2, axis=0)))
    cw = cw_ref[...]
    conv = cw[0:1, :] * y2 + cw[1:2, :] * y1 + cw[2:3, :] * y
    o_ref[...] = (cb * conv).astype(o_ref.dtype)
    return y[tm - CONV_HALO:, :]


def _mix_in_kernel(x_ref, g_ref, w_ref, lng_ref, lnb_ref, ws_ref, bs_ref, cw_ref,
                   qkv_ref, yb_ref, yc_ref, halo_ref, *, tiles_per_seq):
    wd = yb_ref.shape[1]
    c_qkv, c_sgu = 3 * wd, 5 * wd
    halo = jnp.where(pl.program_id(0) % tiles_per_seq == 0, 0.0, halo_ref[...])
    subs = _sub_tiles(x_ref.shape[0])
    proj = []
    for rows in subs:
        h = _rms(x_ref[rows, :], g_ref[...]).astype(BF16)
        qkv_ref[rows, :] = _dot(h, w_ref[:, :c_qkv]).astype(qkv_ref.dtype)
        proj.append((_dot(h, w_ref[:, c_qkv:c_sgu]), _dot(h, w_ref[:, c_sgu:])))
    for rows, (z, p) in zip(subs, proj):
        _spatial_gating(z, lng_ref, lnb_ref, ws_ref, bs_ref, yb_ref.at[rows, :])
        halo = _gated_conv(p, cw_ref, yc_ref.at[rows, :], halo)
    halo_ref[...] = halo


def _mix_in(x, g, w_in, ln_g, ln_b, w_s, b_s_t, conv_w, seq, layer, name):
    rows, d = x.shape
    wd = ln_g.shape[2]
    tm = MIX_TM
    row_w = pl.BlockSpec((tm, wd), lambda i: (i, 0))
    return pl.pallas_call(
        functools.partial(_mix_in_kernel, tiles_per_seq=seq // tm),
        out_shape=(jax.ShapeDtypeStruct((rows, 3 * wd), BF16),
                   jax.ShapeDtypeStruct((rows, wd), BF16),
                   jax.ShapeDtypeStruct((rows, wd), BF16)),
        grid=(rows // tm,),
        in_specs=[
            pl.BlockSpec((tm, d), lambda i: (i, 0)),
            _layer_spec((1, d), layer),
            _layer_spec((d, 8 * wd), layer),
            _layer_spec((1, wd), layer),
            _layer_spec((1, wd), layer),
            _layer_spec((SGU_GROUPS, SGU_LEN, SGU_LEN), layer),
            _layer_spec((SGU_LEN, SGU_GROUPS), layer),
            _layer_spec((CONV_WIDTH, wd), layer),
        ],
        out_specs=(pl.BlockSpec((tm, 3 * wd), lambda i: (i, 0)), row_w, row_w),
        scratch_shapes=[pltpu.VMEM((CONV_HALO, wd), F32)],
        compiler_params=_params("arbitrary"),
        name=name,
    )(x, g, w_in, ln_g, ln_b, w_s, b_s_t, conv_w)


def _sb_attn_kernel(q_ref, k_ref, v_ref, tri_ref, o_ref, acc_ref, carry_ref):
    tq, kg = ATT_TQ, ATT_KG
    n_pairs = q_ref.shape[1] // LANES
    n_qb = q_ref.shape[0] // tq
    first_qb = pl.program_id(1) * n_qb
    lane = lax.broadcasted_iota(jnp.int32, (tq, LANES), 1)
    row = lax.broadcasted_iota(jnp.int32, (2 * tq, kg), 0)
    col = lax.broadcasted_iota(jnp.int32, (2 * tq, kg), 1)
    diag_bias = jnp.where(col < jnp.where(row >= tq, row - tq, row), 0.0, MASKED_LOG_WEIGHT)

    def stacked_queries(qb, p):
        q = q_ref[qb * tq:(qb + 1) * tq, p * LANES:(p + 1) * LANES] * jnp.asarray(SB_HEAD_DIM ** -0.5, BF16)
        zero = jnp.zeros_like(q)
        return jnp.concatenate([jnp.where(lane < SB_HEAD_DIM, q, zero),
                                jnp.where(lane >= SB_HEAD_DIM, q, zero)], axis=0)

    def group(qb, p, g, bias=None, first=False):
        keys = pl.ds(pl.multiple_of(g * kg, kg), kg)
        z = _dot_nt(q2s[qb][p], k_ref[keys, p * LANES:(p + 1) * LANES])
        if bias is not None:
            z = z + bias
        sp = jnp.log(1.0 + jnp.exp2(jnp.abs(z) * -LOG2E))
        log_sig = jnp.minimum(z, 0.0) - sp
        log_1m = log_sig - z
        suffix = _dot(log_1m.astype(BF16), tri_ref[...])
        total = jnp.sum(log_1m, axis=-1, keepdims=True)
        vs = v_ref[keys, p * LANES:(p + 1) * LANES]
        if first:
            acc_ref[qb, p] = _dot(jnp.exp(log_sig + suffix).astype(BF16), vs)
            carry_ref[qb, p] = total
        else:
            carry = carry_ref[qb, p]
            acc_ref[qb, p] += _dot(jnp.exp(log_sig + suffix + carry).astype(BF16), vs)
            carry_ref[qb, p] = carry + total

    q2s = [[stacked_queries(qb, p) for p in range(n_pairs)] for qb in range(n_qb)]
    for qb in range(n_qb):
        for p in range(n_pairs):
            group(qb, p, first_qb + qb, diag_bias, first=True)
    for qb in range(n_qb):
        i = first_qb + qb
        bias = None if qb > 0 else jnp.where(i >= 1, 0.0, MASKED_LOG_WEIGHT)
        for p in range(n_pairs):
            group(qb, p, jnp.maximum(i - 1, 0), bias)

    @pl.when(jnp.logical_and(first_qb + n_qb - 1 >= 2, jnp.max(carry_ref[...]) > DEAD_LOG_WEIGHT))
    def _():
        for qb in range(n_qb):
            for p in range(n_pairs):
                def step(g, qb=qb, p=p):
                    group(qb, p, g)
                    return g - 1

                lax.while_loop(
                    lambda g, qb=qb, p=p: jnp.logical_and(g >= 0, jnp.max(carry_ref[qb, p]) > DEAD_LOG_WEIGHT),
                    step, first_qb + qb - 2)

    for qb in range(n_qb):
        for p in range(n_pairs):
            acc = acc_ref[qb, p]
            o_ref[qb * tq:(qb + 1) * tq, p * LANES:(p + 1) * LANES] = jnp.where(
                lane < SB_HEAD_DIM, acc[:tq], acc[tq:]).astype(o_ref.dtype)


def _tri_matrix():
    j = lax.broadcasted_iota(jnp.int32, (ATT_KG, ATT_KG), 0)
    s = lax.broadcasted_iota(jnp.int32, (ATT_KG, ATT_KG), 1)
    return (j > s).astype(BF16)


def _sb_attention(qkv, name):
    b, s, _ = qkv.shape
    tq = ATT_TM
    width = SB_HEADS * SB_HEAD_DIM
    pairs = width // LANES
    return pl.pallas_call(
        _sb_attn_kernel,
        out_shape=jax.ShapeDtypeStruct((b, s, width), BF16),
        grid=(b, s // tq),
        in_specs=[
            pl.BlockSpec((None, tq, width), lambda bi, i: (bi, i, 0)),
            pl.BlockSpec((None, s, width), lambda bi, i: (bi, 0, 1)),
            pl.BlockSpec((None, s, width), lambda bi, i: (bi, 0, 2)),
            pl.BlockSpec((ATT_KG, ATT_KG), lambda bi, i: (0, 0)),
        ],
        out_specs=pl.BlockSpec((None, tq, width), lambda bi, i: (bi, i, 0)),
        scratch_shapes=[
            pltpu.VMEM((tq // ATT_TQ, pairs, 2 * ATT_TQ, LANES), F32),
            pltpu.VMEM((tq // ATT_TQ, pairs, 2 * ATT_TQ, 1), F32),
        ],
        compiler_params=_params("parallel", "arbitrary"),
        name=name,
    )(qkv, qkv, qkv, _tri_matrix())


def _merge_kernel(x_ref, g_ref, wga_ref, wgb_ref, wgc_ref, ya_ref, yb_ref, yc_ref, wb_ref, wo_ref, o_ref):
    wb, wo = wb_ref[...].astype(BF16), wo_ref[...].astype(BF16)
    for rows in _sub_tiles(x_ref.shape[0]):
        x = x_ref[rows, :]
        h = _rms(x, g_ref[...]).astype(BF16)
        merged = None
        for n, (wg_ref, y_ref) in enumerate(((wga_ref, ya_ref), (wgb_ref, yb_ref), (wgc_ref, yc_ref))):
            gate = jax.nn.sigmoid(_dot(h, wg_ref[...]))
            term = gate * _dot(y_ref[rows, :], wb[n])
            merged = term if merged is None else merged + term
        o_ref[rows, :] = x + _dot(merged.astype(BF16), wo)


def _merge(x, g, w_in, ya, yb, yc, w_branch, w_out, layer, name):
    rows, d = x.shape
    wd = ya.shape[1]
    tm = MERGE_TM
    first_gate_block = (w_in.shape[2] - N_BRANCHES * d) // d
    row_d = pl.BlockSpec((tm, d), lambda i: (i, 0))
    row_w = pl.BlockSpec((tm, wd), lambda i: (i, 0))
    gate_w = [pl.BlockSpec((None, d, d), lambda i, n=n: (layer, 0, first_gate_block + n),
                           pipeline_mode=pl.Buffered(1)) for n in range(N_BRANCHES)]
    return pl.pallas_call(
        _merge_kernel,
        out_shape=jax.ShapeDtypeStruct((rows, d), F32),
        grid=(rows // tm,),
        in_specs=[
            row_d,
            _layer_spec((1, d), layer),
            *gate_w,
            row_w, row_w, row_w,
            _layer_spec((N_BRANCHES, wd, d), layer),
            _layer_spec((d, d), layer),
        ],
        out_specs=row_d,
        compiler_params=_params("parallel"),
        name=name,
    )(x, g, w_in, w_in, w_in, ya, yb, yc, w_branch, w_out)


def _xattn_kernel(x_ref, g_ref, wq_ref, k_ref, v_ref, wo_ref, o_ref):
    d = x_ref.shape[1]
    dh = d // XA_HEADS
    wq, wo = wq_ref[...].astype(BF16), wo_ref[...].astype(BF16)
    for rows in _sub_tiles(x_ref.shape[0]):
        x = x_ref[rows, :]
        h = _rms(x, g_ref[...]).astype(BF16)
        q = (_dot(h, wq) * (dh ** -0.5)).astype(BF16)
        heads = []
        for hd in range(XA_HEADS):
            cols = slice(hd * dh, (hd + 1) * dh)
            s = _dot_nt(q[:, cols], k_ref[:, cols])
            p = jnp.exp(s - jnp.max(s, axis=-1, keepdims=True))
            p = p / jnp.sum(p, axis=-1, keepdims=True)
            heads.append(_dot(p.astype(BF16), v_ref[:, cols]).astype(BF16))
        o_ref[rows, :] = x + _dot(jnp.concatenate(heads, axis=1), wo)


def _xattn(x, g, wq, k, v, wo, layer, name):
    b, s, d = x.shape
    mem = k.shape[1]
    tm = XA_TM
    row_d = pl.BlockSpec((None, tm, d), lambda bi, i: (bi, i, 0))
    return pl.pallas_call(
        _xattn_kernel,
        out_shape=jax.ShapeDtypeStruct((b, s, d), F32),
        grid=(b, s // tm),
        in_specs=[
            row_d,
            _layer_spec((1, d), layer),
            _layer_spec((d, d), layer),
            pl.BlockSpec((None, mem, d), lambda bi, i: (bi, 0, 0)),
            pl.BlockSpec((None, mem, d), lambda bi, i: (bi, 0, 0)),
            _layer_spec((d, d), layer),
        ],
        out_specs=row_d,
        compiler_params=_params("parallel", "parallel"),
        name=name,
    )(x, g, wq, k, v, wo)


def _ffn_kernel(x_ref, g_ref, wg_ref, wu_ref, wd_ref, fg_ref, o_ref, h_ref, acc_ref, *, final_norm):
    subs = _sub_tiles(x_ref.shape[0])
    for rows in subs:
        h_ref[rows, :] = _rms(x_ref[rows, :], g_ref[...]).astype(BF16)
    for c in range(0, wg_ref.shape[1], FFN_TH):
        cols = slice(c, c + FFN_TH)
        for rows in subs:
            h = h_ref[rows, :]
            a = _dot(h, wg_ref[:, cols])
            t = ((a * jax.nn.sigmoid(a)) * _dot(h, wu_ref[:, cols])).astype(BF16)
            if c == 0:
                acc_ref[rows, :] = _dot(t, wd_ref[cols, :])
            else:
                acc_ref[rows, :] += _dot(t, wd_ref[cols, :])
    for rows in subs:
        y = x_ref[rows, :] + acc_ref[rows, :]
        if final_norm:
            y = _rms(y, fg_ref[...])
        o_ref[rows, :] = y


def _ffn(x, g, w_gate, w_up, w_down, final_g, final_norm, layer, name):
    rows, d = x.shape
    hid = w_gate.shape[2]
    tm = FFN_TM
    row_d = pl.BlockSpec((tm, d), lambda i: (i, 0))
    return pl.pallas_call(
        functools.partial(_ffn_kernel, final_norm=final_norm),
        out_shape=jax.ShapeDtypeStruct((rows, d), F32),
        grid=(rows // tm,),
        in_specs=[
            row_d,
            _layer_spec((1, d), layer),
            _layer_spec((d, hid), layer),
            _layer_spec((d, hid), layer),
            _layer_spec((hid, d), layer),
            pl.BlockSpec((1, d), lambda i: (0, 0)),
        ],
        out_specs=row_d,
        scratch_shapes=[pltpu.VMEM((tm, d), BF16), pltpu.VMEM((tm, d), F32)],
        compiler_params=_params("parallel"),
        name=name,
    )(x, g, w_gate, w_up, w_down, final_g)


def kernel(x, mem, norm_mix_g, w_in, sgu_ln_g, sgu_ln_b, w_spatial, b_spatial, conv_w, w_branch, w_out, norm_xa_g, mem_norm_g, w_q_xa, w_k_xa, w_v_xa, w_o_xa, norm_ffn_g, w_gate_ffn, w_up_ffn, w_down_ffn, final_g):
    b, s, d = x.shape
    depth = w_in.shape[0]
    wd = w_branch.shape[2]
    n_mem = mem.shape[1]
    hid = w_gate_ffn.shape[2]
    assert d % LANES == 0 and s % max(ATT_TM, MIX_TM, MERGE_TM, XA_TM) == 0
    assert ATT_KG == ATT_TQ and ATT_KG % LANES == 0 and ATT_TM % ATT_TQ == 0 and SUB_TM % SGU_LEN == 0
    assert all(t % SUB_TM == 0 for t in (MIX_TM, MERGE_TM, XA_TM, FFN_TM))
    assert (b * s) % FFN_TM == 0 and wd == SB_HEADS * SB_HEAD_DIM
    assert hid % FFN_TH == 0 and w_in.shape[2] == 8 * wd + N_BRANCHES * d

    rows3 = lambda v: v.reshape(depth, 1, -1)
    w_in_bf = w_in.astype(BF16)
    w_gate_bf, w_up_bf, w_down_bf = (w.astype(BF16) for w in (w_gate_ffn, w_up_ffn, w_down_ffn))
    g_mix, g_xa, g_mem, g_ffn = (rows3(g) for g in (norm_mix_g, norm_xa_g, mem_norm_g, norm_ffn_g))
    ln_g, ln_b = rows3(sgu_ln_g), rows3(sgu_ln_b)
    b_s_t = jnp.swapaxes(b_spatial, 1, 2)

    xf = x.reshape(b * s, d)
    memf = mem.reshape(b * n_mem, d)
    for l in range(depth):
        qkv, yb, yc = _mix_in(xf, g_mix, w_in_bf, ln_g, ln_b, w_spatial, b_s_t, conv_w, s, l, f"mix_in_{l}")
        ya = _sb_attention(qkv.reshape(b, s, 3 * wd), f"sb_attn_{l}").reshape(b * s, wd)
        xf = _merge(xf, g_mix, w_in_bf, ya, yb, yc, w_branch, w_out, l, f"merge_{l}")
        k_mem, v_mem = (t.reshape(b, n_mem, d) for t in _mem_kv(memf, g_mem, w_k_xa, w_v_xa, l, f"mem_kv_{l}"))
        xf = _xattn(xf.reshape(b, s, d), g_xa, w_q_xa, k_mem, v_mem, w_o_xa, l, f"xattn_{l}").reshape(b * s, d)
        xf = _ffn(xf, g_ffn, w_gate_bf, w_up_bf, w_down_bf, final_g.reshape(1, d), l == depth - 1, l, f"ffn_{l}")
    return xf.reshape(b, s, d)
```

```python
import functools

import jax
import jax.numpy as jnp
from jax import lax
from jax.experimental import pallas as pl
from jax.experimental.pallas import tpu as pltpu

F32 = jnp.float32
BF16 = jnp.bfloat16

LANES = 128
V7X_VMEM_LIMIT_BYTES = 56 * 1024 * 1024

SB_HEADS = 8
SB_HEAD_DIM = 64
SGU_LEN = 128
SGU_GROUPS = 4
CHUNK = 64
CONV_WIDTH = 3
N_BRANCHES = 3
XA_HEADS = 4
RMS_EPS = 1e-6
LN_EPS = 1e-5
LOG2E = 1.4426950408889634
MASKED_LOG_WEIGHT = -1e30
DEAD_LOG_WEIGHT = -120.0

PROJ_TM = 1024
PROJ_TN = 512
SUB_TM = 512
MIX_TM = 1024
ATT_TQ = 256
ATT_TM = 512
ATT_KG = 256
MERGE_TM = 1024
XA_TM = 2048
FFN_TM = 1024
FFN_TH = 256
CONV_HALO = 8


def _params(*semantics):
    return pltpu.CompilerParams(dimension_semantics=semantics,
                                vmem_limit_bytes=V7X_VMEM_LIMIT_BYTES)


def _layer_spec(shape, layer):
    zeros = (0,) * len(shape)
    return pl.BlockSpec((None, *shape), lambda *_: (layer, *zeros), pipeline_mode=pl.Buffered(1))


def _sub_tiles(n_rows):
    return [slice(r, r + SUB_TM) for r in range(0, n_rows, SUB_TM)]


def _rms(x, g):
    return x * lax.rsqrt(jnp.mean(x * x, axis=-1, keepdims=True) + RMS_EPS) * g


def _dot(a, b):
    return jnp.dot(a, b, preferred_element_type=F32)


def _dot_nt(a, b):
    return lax.dot_general(a, b, (((1,), (1,)), ((), ())), preferred_element_type=F32)


def _mem_kv_kernel(x_ref, g_ref, wk_ref, wv_ref, k_ref, v_ref, h_ref):
    @pl.when(pl.program_id(1) == 0)
    def _():
        h_ref[...] = _rms(x_ref[...], g_ref[...]).astype(BF16)

    h = h_ref[...]
    k_ref[...] = _dot(h, wk_ref[...].astype(BF16)).astype(k_ref.dtype)
    v_ref[...] = _dot(h, wv_ref[...].astype(BF16)).astype(v_ref.dtype)


def _mem_kv(x, g, wk, wv, layer, name):
    rows, d = x.shape
    cols = wk.shape[2]
    tm = min(PROJ_TM, rows)
    w_spec = pl.BlockSpec((None, d, PROJ_TN), lambda i, j: (layer, 0, j))
    o_spec = pl.BlockSpec((tm, PROJ_TN), lambda i, j: (i, j))
    return pl.pallas_call(
        _mem_kv_kernel,
        out_shape=(jax.ShapeDtypeStruct((rows, cols), BF16), jax.ShapeDtypeStruct((rows, cols), BF16)),
        grid=(rows // tm, cols // PROJ_TN),
        in_specs=[
            pl.BlockSpec((tm, d), lambda i, j: (i, 0)),
            _layer_spec((1, d), layer),
            w_spec, w_spec,
        ],
        out_specs=(o_spec, o_spec),
        scratch_shapes=[pltpu.VMEM((tm, d), BF16)],
        compiler_params=_params("parallel", "arbitrary"),
        name=name,
    )(x, g, wk, wv)


def _spatial_gating(z, lng_ref, lnb_ref, ws_ref, bs_ref, o_ref):
    tm = z.shape[0]
    wd = o_ref.shape[1]
    gd = wd // SGU_GROUPS
    z = 0.5 * z * (1.0 + lax.erf(z * (2.0 ** -0.5)))
    u = z[:, :wd]
    v = z[:, wd:]
    mu = jnp.mean(v, axis=-1, keepdims=True)
    vc = v - mu
    v = vc * lax.rsqrt(jnp.mean(vc * vc, axis=-1, keepdims=True) + LN_EPS)
    v = (v * lng_ref[...] + lnb_ref[...]).astype(BF16)
    pos_t = lax.broadcasted_iota(jnp.int32, (SGU_LEN, SGU_LEN), 0)
    pos_s = lax.broadcasted_iota(jnp.int32, (SGU_LEN, SGU_LEN), 1)
    allowed = (pos_t // CHUNK) >= (pos_s // CHUNK)
    for gi in range(SGU_GROUPS):
        wm = jnp.where(allowed, ws_ref[gi], 0.0).astype(BF16)
        bias = bs_ref[:, gi:gi + 1]
        for c in range(tm // SGU_LEN):
            rows = slice(c * SGU_LEN, (c + 1) * SGU_LEN)
            cols = slice(gi * gd, (gi + 1) * gd)
            vm = _dot(wm, v[rows, cols]) + bias
            o_ref[rows, cols] = (u[rows, cols] * vm).astype(o_ref.dtype)


def _gated_conv(p, cw_ref, o_ref, halo):
    tm = p.shape[0]
    wd = o_ref.shape[1]
    cb = p[:, :wd]
    y = p[:, wd:2 * wd] * p[:, 2 * wd:]
    prev1 = halo[CONV_HALO - 1:CONV_HALO, :]
    prev2 = halo[CONV_HALO - 2:CONV_HALO - 1, :]
    row = lax.broadcasted_iota(jnp.int32, (tm, wd), 0)
    y1 = jnp.where(row == 0, prev1, pltpu.roll(y, 1, axis=0))
    y2 = jnp.where(row == 0, prev2, jnp.where(row == 1, prev1, pltpu.roll(y, 2, axis=0)))
    cw = cw_ref[...]
    conv = cw[0:1, :] * y2 + cw[1:2, :] * y1 + cw[2:3, :] * y
    o_ref[...] = (cb * conv).astype(o_ref.dtype)
    return y[tm - CONV_HALO:, :]


def _mix_in_kernel(x_ref, g_ref, w_ref, lng_ref, lnb_ref, ws_ref, bs_ref, cw_ref,
                   qkv_ref, yb_ref, yc_ref, halo_ref, *, tiles_per_seq):
    wd = yb_ref.shape[1]
    c_qkv, c_sgu = 3 * wd, 5 * wd
    halo = jnp.where(pl.program_id(0) % tiles_per_seq == 0, 0.0, halo_ref[...])
    subs = _sub_tiles(x_ref.shape[0])
    proj = []
    for rows in subs:
        h = _rms(x_ref[rows, :], g_ref[...]).astype(BF16)
        qkv_ref[rows, :] = _dot(h, w_ref[:, :c_qkv]).astype(qkv_ref.dtype)
        proj.append((_dot(h, w_ref[:, c_qkv:c_sgu]), _dot(h, w_ref[:, c_sgu:])))
    for rows, (z, p) in zip(subs, proj):
        _spatial_gating(z, lng_ref, lnb_ref, ws_ref, bs_ref, yb_ref.at[rows, :])
        halo = _gated_conv(p, cw_ref, yc_ref.at[rows, :], halo)
    halo_ref[...] = halo


def _mix_in(x, g, w_in, ln_g, ln_b, w_s, b_s_t, conv_w, seq, layer, w_layer, name):
    rows, d = x.shape
    wd = ln_g.shape[2]
    tm = MIX_TM
    row_w = pl.BlockSpec((tm, wd), lambda i: (i, 0))
    return pl.pallas_call(
        functools.partial(_mix_in_kernel, tiles_per_seq=seq // tm),
        out_shape=(jax.ShapeDtypeStruct((rows, 3 * wd), BF16),
                   jax.ShapeDtypeStruct((rows, wd), BF16),
                   jax.ShapeDtypeStruct((rows, wd), BF16)),
        grid=(rows // tm,),
        in_specs=[
            pl.BlockSpec((tm, d), lambda i: (i, 0)),
            _layer_spec((1, d), layer),
            _layer_spec((d, 8 * wd), w_layer),
            _layer_spec((1, wd), layer),
            _layer_spec((1, wd), layer),
            _layer_spec((SGU_GROUPS, SGU_LEN, SGU_LEN), layer),
            _layer_spec((SGU_LEN, SGU_GROUPS), layer),
            _layer_spec((CONV_WIDTH, wd), layer),
        ],
        out_specs=(pl.BlockSpec((tm, 3 * wd), lambda i: (i, 0)), row_w, row_w),
        scratch_shapes=[pltpu.VMEM((CONV_HALO, wd), F32)],
        compiler_params=_params("arbitrary"),
        name=name,
    )(x, g, w_in, ln_g, ln_b, w_s, b_s_t, conv_w)


def _sb_attn_kernel(q_ref, k_ref, v_ref, tri_ref, *refs, n_riders):
    rider_in, o_ref, rider_out = refs[:n_riders], refs[n_riders], refs[n_riders + 1:2 * n_riders + 1]
    acc_ref, carry_ref = refs[2 * n_riders + 1:]
    for src, dst in zip(rider_in, rider_out):
        dst[...] = src[...].astype(BF16)
    tq, kg = ATT_TQ, ATT_KG
    n_pairs = q_ref.shape[1] // LANES
    n_qb = q_ref.shape[0] // tq
    first_qb = pl.program_id(1) * n_qb
    lane = lax.broadcasted_iota(jnp.int32, (tq, LANES), 1)
    row = lax.broadcasted_iota(jnp.int32, (2 * tq, kg), 0)
    col = lax.broadcasted_iota(jnp.int32, (2 * tq, kg), 1)
    diag_bias = jnp.where(col < jnp.where(row >= tq, row - tq, row), 0.0, MASKED_LOG_WEIGHT)

    def stacked_queries(qb, p):
        q = q_ref[qb * tq:(qb + 1) * tq, p * LANES:(p + 1) * LANES] * jnp.asarray(SB_HEAD_DIM ** -0.5, BF16)
        zero = jnp.zeros_like(q)
        return jnp.concatenate([jnp.where(lane < SB_HEAD_DIM, q, zero),
                                jnp.where(lane >= SB_HEAD_DIM, q, zero)], axis=0)

    def group(qb, p, g, bias=None, first=False):
        keys = pl.ds(pl.multiple_of(g * kg, kg), kg)
        z = _dot_nt(q2s[qb][p], k_ref[keys, p * LANES:(p + 1) * LANES])
        if bias is not None:
            z = z + bias
        sp = jnp.log(1.0 + jnp.exp2(jnp.abs(z) * -LOG2E))
        log_sig = jnp.minimum(z, 0.0) - sp
        log_1m = log_sig - z
        suffix = _dot(log_1m.astype(BF16), tri_ref[...])
        total = jnp.sum(log_1m, axis=-1, keepdims=True)
        vs = v_ref[keys, p * LANES:(p + 1) * LANES]
        if first:
            acc_ref[qb, p] = _dot(jnp.exp(log_sig + suffix).astype(BF16), vs)
            carry_ref[qb, p] = total
        else:
            carry = carry_ref[qb, p]
            acc_ref[qb, p] += _dot(jnp.exp(log_sig + suffix + carry).astype(BF16), vs)
            carry_ref[qb, p] = carry + total

    q2s = [[stacked_queries(qb, p) for p in range(n_pairs)] for qb in range(n_qb)]
    for qb in range(n_qb):
        for p in range(n_pairs):
            group(qb, p, first_qb + qb, diag_bias, first=True)
    for qb in range(n_qb):
        i = first_qb + qb
        bias = None if qb > 0 else jnp.where(i >= 1, 0.0, MASKED_LOG_WEIGHT)
        for p in range(n_pairs):
            group(qb, p, jnp.maximum(i - 1, 0), bias)

    @pl.when(jnp.logical_and(first_qb + n_qb - 1 >= 2, jnp.max(carry_ref[...]) > DEAD_LOG_WEIGHT))
    def _():
        for qb in range(n_qb):
            for p in range(n_pairs):
                def step(g, qb=qb, p=p):
                    group(qb, p, g)
                    return g - 1

                lax.while_loop(
                    lambda g, qb=qb, p=p: jnp.logical_and(g >= 0, jnp.max(carry_ref[qb, p]) > DEAD_LOG_WEIGHT),
                    step, first_qb + qb - 2)

    for qb in range(n_qb):
        for p in range(n_pairs):
            acc = acc_ref[qb, p]
            o_ref[qb * tq:(qb + 1) * tq, p * LANES:(p + 1) * LANES] = jnp.where(
                lane < SB_HEAD_DIM, acc[:tq], acc[tq:]).astype(o_ref.dtype)


def _tri_matrix():
    j = lax.broadcasted_iota(jnp.int32, (ATT_KG, ATT_KG), 0)
    s = lax.broadcasted_iota(jnp.int32, (ATT_KG, ATT_KG), 1)
    return (j > s).astype(BF16)


def _sb_attention_riders(qkv, riders, layers, name):
    b, s, _ = qkv.shape
    tq = ATT_TM
    width = SB_HEADS * SB_HEAD_DIM
    pairs = width // LANES
    steps_per_batch = s // tq
    n_steps = b * steps_per_batch
    split = [w.reshape(w.shape[0], n_steps, w.shape[1] // n_steps, w.shape[2]) for w in riders]
    rider_specs = [pl.BlockSpec((None, None, *w.shape[2:]), lambda bi, i, la=la: (la, bi * steps_per_batch + i, 0, 0))
                   for w, la in zip(split, layers)]
    rider_out_specs = [pl.BlockSpec((None, *w.shape[2:]), lambda bi, i: (bi * steps_per_batch + i, 0, 0))
                       for w in split]
    outs = pl.pallas_call(
        functools.partial(_sb_attn_kernel, n_riders=len(riders)),
        out_shape=(jax.ShapeDtypeStruct((b, s, width), BF16),
                   *[jax.ShapeDtypeStruct(w.shape[1:], BF16) for w in split]),
        grid=(b, steps_per_batch),
        in_specs=[
            pl.BlockSpec((None, tq, width), lambda bi, i: (bi, i, 0)),
            pl.BlockSpec((None, s, width), lambda bi, i: (bi, 0, 1)),
            pl.BlockSpec((None, s, width), lambda bi, i: (bi, 0, 2)),
            pl.BlockSpec((ATT_KG, ATT_KG), lambda bi, i: (0, 0)),
            *rider_specs,
        ],
        out_specs=(pl.BlockSpec((None, tq, width), lambda bi, i: (bi, i, 0)), *rider_out_specs),
        scratch_shapes=[
            pltpu.VMEM((tq // ATT_TQ, pairs, 2 * ATT_TQ, LANES), F32),
            pltpu.VMEM((tq // ATT_TQ, pairs, 2 * ATT_TQ, 1), F32),
        ],
        compiler_params=_params("parallel", "arbitrary"),
        name=name,
    )(qkv, qkv, qkv, _tri_matrix(), *split)
    return outs[0], [o.reshape(1, *w.shape[1:]) for o, w in zip(outs[1:], riders)]


def _merge_kernel(x_ref, g_ref, wga_ref, wgb_ref, wgc_ref, ya_ref, yb_ref, yc_ref, wb_ref, wo_ref, o_ref):
    wb, wo = wb_ref[...].astype(BF16), wo_ref[...].astype(BF16)
    for rows in _sub_tiles(x_ref.shape[0]):
        x = x_ref[rows, :]
        h = _rms(x, g_ref[...]).astype(BF16)
        merged = None
        for n, (wg_ref, y_ref) in enumerate(((wga_ref, ya_ref), (wgb_ref, yb_ref), (wgc_ref, yc_ref))):
            gate = jax.nn.sigmoid(_dot(h, wg_ref[...]))
            term = gate * _dot(y_ref[rows, :], wb[n])
            merged = term if merged is None else merged + term
        o_ref[rows, :] = x + _dot(merged.astype(BF16), wo)


def _merge(x, g, w_in, ya, yb, yc, w_branch, w_out, layer, w_layer, name):
    rows, d = x.shape
    wd = ya.shape[1]
    tm = MERGE_TM
    first_gate_block = (w_in.shape[2] - N_BRANCHES * d) // d
    row_d = pl.BlockSpec((tm, d), lambda i: (i, 0))
    row_w = pl.BlockSpec((tm, wd), lambda i: (i, 0))
    gate_w = [pl.BlockSpec((None, d, d), lambda i, n=n: (w_layer, 0, first_gate_block + n),
                           pipeline_mode=pl.Buffered(1)) for n in range(N_BRANCHES)]
    return pl.pallas_call(
        _merge_kernel,
        out_shape=jax.ShapeDtypeStruct((rows, d), F32),
        grid=(rows // tm,),
        in_specs=[
            row_d,
            _layer_spec((1, d), layer),
            *gate_w,
            row_w, row_w, row_w,
            _layer_spec((N_BRANCHES, wd, d), layer),
            _layer_spec((d, d), layer),
        ],
        out_specs=row_d,
        compiler_params=_params("parallel"),
        name=name,
    )(x, g, w_in, w_in, w_in, ya, yb, yc, w_branch, w_out)


def _xattn_kernel(x_ref, g_ref, wq_ref, k_ref, v_ref, wo_ref, o_ref):
    d = x_ref.shape[1]
    dh = d // XA_HEADS
    wq, wo = wq_ref[...].astype(BF16), wo_ref[...].astype(BF16)
    for rows in _sub_tiles(x_ref.shape[0]):
        x = x_ref[rows, :]
        h = _rms(x, g_ref[...]).astype(BF16)
        q = (_dot(h, wq) * (dh ** -0.5)).astype(BF16)
        heads = []
        for hd in range(XA_HEADS):
            cols = slice(hd * dh, (hd + 1) * dh)
            s = _dot_nt(q[:, cols], k_ref[:, cols])
            p = jnp.exp(s - jnp.max(s, axis=-1, keepdims=True))
            p = p / jnp.sum(p, axis=-1, keepdims=True)
            heads.append(_dot(p.astype(BF16), v_ref[:, cols]).astype(BF16))
        o_ref[rows, :] = x + _dot(jnp.concatenate(heads, axis=1), wo)


def _xattn(x, g, wq, k, v, wo, layer, name):
    b, s, d = x.shape
    mem = k.shape[1]
    tm = XA_TM
    row_d = pl.BlockSpec((None, tm, d), lambda bi, i: (bi, i, 0))
    return pl.pallas_call(
        _xattn_kernel,
        out_shape=jax.ShapeDtypeStruct((b, s, d), F32),
        grid=(b, s // tm),
        in_specs=[
            row_d,
            _layer_spec((1, d), layer),
            _layer_spec((d, d), layer),
            pl.BlockSpec((None, mem, d), lambda bi, i: (bi, 0, 0)),
            pl.BlockSpec((None, mem, d), lambda bi, i: (bi, 0, 0)),
            _layer_spec((d, d), layer),
        ],
        out_specs=row_d,
        compiler_params=_params("parallel", "parallel"),
        name=name,
    )(x, g, wq, k, v, wo)


def _ffn_kernel(x_ref, g_ref, wg_ref, wu_ref, wd_ref, fg_ref, o_ref, h_ref, acc_ref, *, final_norm):
    subs = _sub_tiles(x_ref.shape[0])
    for rows in subs:
        h_ref[rows, :] = _rms(x_ref[rows, :], g_ref[...]).astype(BF16)
    for c in range(0, wg_ref.shape[1], FFN_TH):
        cols = slice(c, c + FFN_TH)
        for rows in subs:
            h = h_ref[rows, :]
            a = _dot(h, wg_ref[:, cols])
            t = ((a * jax.nn.sigmoid(a)) * _dot(h, wu_ref[:, cols])).astype(BF16)
            if c == 0:
                acc_ref[rows, :] = _dot(t, wd_ref[cols, :])
            else:
                acc_ref[rows, :] += _dot(t, wd_ref[cols, :])
    for rows in subs:
        y = x_ref[rows, :] + acc_ref[rows, :]
        if final_norm:
            y = _rms(y, fg_ref[...])
        o_ref[rows, :] = y


def _ffn(x, g, w_gate, w_up, w_down, final_g, final_norm, layer, w_layer, name):
    rows, d = x.shape
    hid = w_gate.shape[2]
    tm = FFN_TM
    row_d = pl.BlockSpec((tm, d), lambda i: (i, 0))
    return pl.pallas_call(
        functools.partial(_ffn_kernel, final_norm=final_norm),
        out_shape=jax.ShapeDtypeStruct((rows, d), F32),
        grid=(rows // tm,),
        in_specs=[
            row_d,
            _layer_spec((1, d), layer),
            _layer_spec((d, hid), w_layer),
            _layer_spec((d, hid), w_layer),
            _layer_spec((hid, d), w_layer),
            pl.BlockSpec((1, d), lambda i: (0, 0)),
        ],
        out_specs=row_d,
        scratch_shapes=[pltpu.VMEM((tm, d), BF16), pltpu.VMEM((tm, d), F32)],
        compiler_params=_params("parallel"),
        name=name,
    )(x, g, w_gate, w_up, w_down, final_g)


def kernel(x, mem, norm_mix_g, w_in, sgu_ln_g, sgu_ln_b, w_spatial, b_spatial, conv_w, w_branch, w_out, norm_xa_g, mem_norm_g, w_q_xa, w_k_xa, w_v_xa, w_o_xa, norm_ffn_g, w_gate_ffn, w_up_ffn, w_down_ffn, final_g):
    b, s, d = x.shape
    depth = w_in.shape[0]
    wd = w_branch.shape[2]
    n_mem = mem.shape[1]
    hid = w_gate_ffn.shape[2]
    assert d % LANES == 0 and s % max(ATT_TM, MIX_TM, MERGE_TM, XA_TM) == 0
    assert ATT_KG == ATT_TQ and ATT_KG % LANES == 0 and ATT_TM % ATT_TQ == 0 and SUB_TM % SGU_LEN == 0
    assert all(t % SUB_TM == 0 for t in (MIX_TM, MERGE_TM, XA_TM, FFN_TM))
    assert (b * s) % FFN_TM == 0 and wd == SB_HEADS * SB_HEAD_DIM
    assert hid % FFN_TH == 0 and w_in.shape[2] == 8 * wd + N_BRANCHES * d

    rows3 = lambda v: v.reshape(depth, 1, -1)
    g_mix, g_xa, g_mem, g_ffn = (rows3(g) for g in (norm_mix_g, norm_xa_g, mem_norm_g, norm_ffn_g))
    ln_g, ln_b = rows3(sgu_ln_g), rows3(sgu_ln_b)
    b_s_t = jnp.swapaxes(b_spatial, 1, 2)

    xf = x.reshape(b * s, d)
    memf = mem.reshape(b * n_mem, d)
    w_in_bf = w_in[:1].astype(BF16)
    for l in range(depth):
        qkv, yb, yc = _mix_in(xf, g_mix, w_in_bf, ln_g, ln_b, w_spatial, b_s_t, conv_w, s, l, 0, f"mix_in_{l}")
        riders = [w_gate_ffn, w_up_ffn, w_down_ffn] + ([w_in] if l + 1 < depth else [])
        rider_layers = [l, l, l, l + 1]
        ya, cast = _sb_attention_riders(qkv.reshape(b, s, 3 * wd), riders, rider_layers, f"sb_attn_{l}")
        xf = _merge(xf, g_mix, w_in_bf, ya.reshape(b * s, wd), yb, yc, w_branch, w_out, l, 0, f"merge_{l}")
        k_mem, v_mem = (t.reshape(b, n_mem, d) for t in _mem_kv(memf, g_mem, w_k_xa, w_v_xa, l, f"mem_kv_{l}"))
        xf = _xattn(xf.reshape(b, s, d), g_xa, w_q_xa, k_mem, v_mem, w_o_xa, l, f"xattn_{l}").reshape(b * s, d)
        xf = _ffn(xf, g_ffn, cast[0], cast[1], cast[2], final_g.reshape(1, d), l == depth - 1, l, 0, f"ffn_{l}")
        if l + 1 < depth:
            w_in_bf = cast[3]
    return xf.reshape(b, s, d)
```

```python
import functools

import jax
import jax.numpy as jnp
from jax import lax
from jax.experimental import pallas as pl
from jax.experimental.pallas import tpu as pltpu

F32 = jnp.float32
BF16 = jnp.bfloat16

LANES = 128
V7X_VMEM_LIMIT_BYTES = 56 * 1024 * 1024

SB_HEADS = 8
SB_HEAD_DIM = 64
SGU_LEN = 128
SGU_GROUPS = 4
CHUNK = 64
CONV_WIDTH = 3
N_BRANCHES = 3
XA_HEADS = 4
RMS_EPS = 1e-6
LN_EPS = 1e-5
LOG2E = 1.4426950408889634
MASKED_LOG_WEIGHT = -1e30
DEAD_LOG_WEIGHT = -120.0

PROJ_TM = 1024
PROJ_TN = 512
SUB_TM = 512
MIX_TM = 1024
ATT_TQ = 256
ATT_TM = 1024
ATT_KG = 256
MERGE_TM = 1024
XA_TM = 2048
FFN_TM = 1024
FFN_TH = 256
CONV_HALO = 8


def _params(*semantics):
    return pltpu.CompilerParams(dimension_semantics=semantics,
                                vmem_limit_bytes=V7X_VMEM_LIMIT_BYTES)


def _layer_spec(shape, layer):
    zeros = (0,) * len(shape)
    return pl.BlockSpec((None, *shape), lambda *_: (layer, *zeros), pipeline_mode=pl.Buffered(1))


def _sub_tiles(n_rows):
    return [slice(r, r + SUB_TM) for r in range(0, n_rows, SUB_TM)]


def _rms(x, g):
    return x * lax.rsqrt(jnp.mean(x * x, axis=-1, keepdims=True) + RMS_EPS) * g


def _dot(a, b):
    return jnp.dot(a, b, preferred_element_type=F32)


def _dot_nt(a, b):
    return lax.dot_general(a, b, (((1,), (1,)), ((), ())), preferred_element_type=F32)


def _mem_kv_kernel(x_ref, g_ref, wk_ref, wv_ref, k_ref, v_ref, h_ref):
    @pl.when(pl.program_id(1) == 0)
    def _():
        h_ref[...] = _rms(x_ref[...], g_ref[...]).astype(BF16)

    h = h_ref[...]
    k_ref[...] = _dot(h, wk_ref[...].astype(BF16)).astype(k_ref.dtype)
    v_ref[...] = _dot(h, wv_ref[...].astype(BF16)).astype(v_ref.dtype)


def _mem_kv(x, g, wk, wv, layer, name):
    rows, d = x.shape
    cols = wk.shape[2]
    tm = min(PROJ_TM, rows)
    w_spec = pl.BlockSpec((None, d, PROJ_TN), lambda i, j: (layer, 0, j))
    o_spec = pl.BlockSpec((tm, PROJ_TN), lambda i, j: (i, j))
    return pl.pallas_call(
        _mem_kv_kernel,
        out_shape=(jax.ShapeDtypeStruct((rows, cols), BF16), jax.ShapeDtypeStruct((rows, cols), BF16)),
        grid=(rows // tm, cols // PROJ_TN),
        in_specs=[
            pl.BlockSpec((tm, d), lambda i, j: (i, 0)),
            _layer_spec((1, d), layer),
            w_spec, w_spec,
        ],
        out_specs=(o_spec, o_spec),
        scratch_shapes=[pltpu.VMEM((tm, d), BF16)],
        compiler_params=_params("parallel", "arbitrary"),
        name=name,
    )(x, g, wk, wv)


def _spatial_gating(z, lng_ref, lnb_ref, ws_ref, bs_ref, o_ref):
    tm = z.shape[0]
    wd = o_ref.shape[1]
    gd = wd // SGU_GROUPS
    z = 0.5 * z * (1.0 + lax.erf(z * (2.0 ** -0.5)))
    u = z[:, :wd]
    v = z[:, wd:]
    mu = jnp.mean(v, axis=-1, keepdims=True)
    vc = v - mu
    v = vc * lax.rsqrt(jnp.mean(vc * vc, axis=-1, keepdims=True) + LN_EPS)
    v = (v * lng_ref[...] + lnb_ref[...]).astype(BF16)
    pos_t = lax.broadcasted_iota(jnp.int32, (SGU_LEN, SGU_LEN), 0)
    pos_s = lax.broadcasted_iota(jnp.int32, (SGU_LEN, SGU_LEN), 1)
    allowed = (pos_t // CHUNK) >= (pos_s // CHUNK)
    for gi in range(SGU_GROUPS):
        wm = jnp.where(allowed, ws_ref[gi], 0.0).astype(BF16)
        bias = bs_ref[:, gi:gi + 1]
        for c in range(tm // SGU_LEN):
            rows = slice(c * SGU_LEN, (c + 1) * SGU_LEN)
            cols = slice(gi * gd, (gi + 1) * gd)
            vm = _dot(wm, v[rows, cols]) + bias
            o_ref[rows, cols] = (u[rows, cols] * vm).astype(o_ref.dtype)


def _gated_conv(p, cw_ref, o_ref, halo):
    tm = p.shape[0]
    wd = o_ref.shape[1]
    cb = p[:, :wd]
    y = p[:, wd:2 * wd] * p[:, 2 * wd:]
    prev1 = halo[CONV_HALO - 1:CONV_HALO, :]
    prev2 = halo[CONV_HALO - 2:CONV_HALO - 1, :]
    row = lax.broadcasted_iota(jnp.int32, (tm, wd), 0)
    y1 = jnp.where(row == 0, prev1, pltpu.roll(y, 1, axis=0))
    y2 = jnp.where(row == 0, prev2, jnp.where(row == 1, prev1, pltpu.roll(y, 2, axis=0)))
    cw = cw_ref[...]
    conv = cw[0:1, :] * y2 + cw[1:2, :] * y1 + cw[2:3, :] * y
    o_ref[...] = (cb * conv).astype(o_ref.dtype)
    return y[tm - CONV_HALO:, :]


def _mix_in_kernel(x_ref, g_ref, w_ref, lng_ref, lnb_ref, ws_ref, bs_ref, cw_ref,
                   qkv_ref, yb_ref, yc_ref, halo_ref, *, tiles_per_seq):
    wd = yb_ref.shape[1]
    c_qkv, c_sgu = 3 * wd, 5 * wd
    halo = jnp.where(pl.program_id(0) % tiles_per_seq == 0, 0.0, halo_ref[...])
    subs = _sub_tiles(x_ref.shape[0])
    proj = []
    for rows in subs:
        h = _rms(x_ref[rows, :], g_ref[...]).astype(BF16)
        qkv_ref[rows, :] = _dot(h, w_ref[:, :c_qkv]).astype(qkv_ref.dtype)
        proj.append((_dot(h, w_ref[:, c_qkv:c_sgu]), _dot(h, w_ref[:, c_sgu:])))
    for rows, (z, p) in zip(subs, proj):
        _spatial_gating(z, lng_ref, lnb_ref, ws_ref, bs_ref, yb_ref.at[rows, :])
        halo = _gated_conv(p, cw_ref, yc_ref.at[rows, :], halo)
    halo_ref[...] = halo


def _mix_in(x, g, w_in, ln_g, ln_b, w_s, b_s_t, conv_w, seq, layer, w_layer, name):
    rows, d = x.shape
    wd = ln_g.shape[2]
    tm = MIX_TM
    row_w = pl.BlockSpec((tm, wd), lambda i: (i, 0))
    return pl.pallas_call(
        functools.partial(_mix_in_kernel, tiles_per_seq=seq // tm),
        out_shape=(jax.ShapeDtypeStruct((rows, 3 * wd), BF16),
                   jax.ShapeDtypeStruct((rows, wd), BF16),
                   jax.ShapeDtypeStruct((rows, wd), BF16)),
        grid=(rows // tm,),
        in_specs=[
            pl.BlockSpec((tm, d), lambda i: (i, 0)),
            _layer_spec((1, d), layer),
            _layer_spec((d, 8 * wd), w_layer),
            _layer_spec((1, wd), layer),
            _layer_spec((1, wd), layer),
            _layer_spec((SGU_GROUPS, SGU_LEN, SGU_LEN), layer),
            _layer_spec((SGU_LEN, SGU_GROUPS), layer),
            _layer_spec((CONV_WIDTH, wd), layer),
        ],
        out_specs=(pl.BlockSpec((tm, 3 * wd), lambda i: (i, 0)), row_w, row_w),
        scratch_shapes=[pltpu.VMEM((CONV_HALO, wd), F32)],
        compiler_params=_params("arbitrary"),
        name=name,
    )(x, g, w_in, ln_g, ln_b, w_s, b_s_t, conv_w)


def _sb_attn_kernel(q_ref, k_ref, v_ref, tri_ref, *refs, n_riders):
    rider_in, o_ref, rider_out = refs[:n_riders], refs[n_riders], refs[n_riders + 1:2 * n_riders + 1]
    acc_ref, carry_ref = refs[2 * n_riders + 1:]
    for src, dst in zip(rider_in, rider_out):
        dst[...] = src[...].astype(BF16)
    tq, kg = ATT_TQ, ATT_KG
    n_pairs = q_ref.shape[1] // LANES
    n_qb = q_ref.shape[0] // tq
    first_qb = pl.program_id(1) * n_qb
    lane = lax.broadcasted_iota(jnp.int32, (tq, LANES), 1)
    row = lax.broadcasted_iota(jnp.int32, (2 * tq, kg), 0)
    col = lax.broadcasted_iota(jnp.int32, (2 * tq, kg), 1)
    diag_bias = jnp.where(col < jnp.where(row >= tq, row - tq, row), 0.0, MASKED_LOG_WEIGHT)

    def stacked_queries(qb, p):
        q = q_ref[qb * tq:(qb + 1) * tq, p * LANES:(p + 1) * LANES] * jnp.asarray(SB_HEAD_DIM ** -0.5, BF16)
        zero = jnp.zeros_like(q)
        return jnp.concatenate([jnp.where(lane < SB_HEAD_DIM, q, zero),
                                jnp.where(lane >= SB_HEAD_DIM, q, zero)], axis=0)

    def group(qb, p, g, bias=None, first=False):
        keys = pl.ds(pl.multiple_of(g * kg, kg), kg)
        z = _dot_nt(q2s[qb][p], k_ref[keys, p * LANES:(p + 1) * LANES])
        if bias is not None:
            z = z + bias
        sp = jnp.log(1.0 + jnp.exp2(jnp.abs(z) * -LOG2E))
        log_sig = jnp.minimum(z, 0.0) - sp
        log_1m = log_sig - z
        suffix = _dot(log_1m.astype(BF16), tri_ref[...])
        total = jnp.sum(log_1m, axis=-1, keepdims=True)
        vs = v_ref[keys, p * LANES:(p + 1) * LANES]
        if first:
            acc_ref[qb, p] = _dot(jnp.exp(log_sig + suffix).astype(BF16), vs)
            carry_ref[qb, p] = total
        else:
            carry = carry_ref[qb, p]
            acc_ref[qb, p] += _dot(jnp.exp(log_sig + suffix + carry).astype(BF16), vs)
            carry_ref[qb, p] = carry + total

    q2s = [[stacked_queries(qb, p) for p in range(n_pairs)] for qb in range(n_qb)]
    for qb in range(n_qb):
        for p in range(n_pairs):
            group(qb, p, first_qb + qb, diag_bias, first=True)
    for qb in range(n_qb):
        i = first_qb + qb
        bias = None if qb > 0 else jnp.where(i >= 1, 0.0, MASKED_LOG_WEIGHT)
        for p in range(n_pairs):
            group(qb, p, jnp.maximum(i - 1, 0), bias)

    @pl.when(jnp.logical_and(first_qb + n_qb - 1 >= 2, jnp.max(carry_ref[...]) > DEAD_LOG_WEIGHT))
    def _():
        for qb in range(n_qb):
            for p in range(n_pairs):
                def step(g, qb=qb, p=p):
                    group(qb, p, g)
                    return g - 1

                lax.while_loop(
                    lambda g, qb=qb, p=p: jnp.logical_and(g >= 0, jnp.max(carry_ref[qb, p]) > DEAD_LOG_WEIGHT),
                    step, first_qb + qb - 2)

    for qb in range(n_qb):
        for p in range(n_pairs):
            acc = acc_ref[qb, p]
            o_ref[qb * tq:(qb + 1) * tq, p * LANES:(p + 1) * LANES] = jnp.where(
                lane < SB_HEAD_DIM, acc[:tq], acc[tq:]).astype(o_ref.dtype)


def _tri_matrix():
    j = lax.broadcasted_iota(jnp.int32, (ATT_KG, ATT_KG), 0)
    s = lax.broadcasted_iota(jnp.int32, (ATT_KG, ATT_KG), 1)
    return (j > s).astype(BF16)


def _sb_attention_riders(qkv, riders, layers, name):
    b, s, _ = qkv.shape
    tq = ATT_TM
    width = SB_HEADS * SB_HEAD_DIM
    pairs = width // LANES
    steps_per_batch = s // tq
    n_steps = b * steps_per_batch
    split = [w.reshape(w.shape[0], n_steps, w.shape[1] // n_steps, w.shape[2]) for w in riders]
    rider_specs = [pl.BlockSpec((None, None, *w.shape[2:]), lambda bi, i, la=la: (la, bi * steps_per_batch + i, 0, 0))
                   for w, la in zip(split, layers)]
    rider_out_specs = [pl.BlockSpec((None, *w.shape[2:]), lambda bi, i: (bi * steps_per_batch + i, 0, 0))
                       for w in split]
    outs = pl.pallas_call(
        functools.partial(_sb_attn_kernel, n_riders=len(riders)),
        out_shape=(jax.ShapeDtypeStruct((b, s, width), BF16),
                   *[jax.ShapeDtypeStruct(w.shape[1:], BF16) for w in split]),
        grid=(b, steps_per_batch),
        in_specs=[
            pl.BlockSpec((None, tq, width), lambda bi, i: (bi, i, 0)),
            pl.BlockSpec((None, s, width), lambda bi, i: (bi, 0, 1)),
            pl.BlockSpec((None, s, width), lambda bi, i: (bi, 0, 2)),
            pl.BlockSpec((ATT_KG, ATT_KG), lambda bi, i: (0, 0)),
            *rider_specs,
        ],
        out_specs=(pl.BlockSpec((None, tq, width), lambda bi, i: (bi, i, 0)), *rider_out_specs),
        scratch_shapes=[
            pltpu.VMEM((tq // ATT_TQ, pairs, 2 * ATT_TQ, LANES), F32),
            pltpu.VMEM((tq // ATT_TQ, pairs, 2 * ATT_TQ, 1), F32),
        ],
        compiler_params=_params("parallel", "arbitrary"),
        name=name,
    )(qkv, qkv, qkv, _tri_matrix(), *split)
    return outs[0], [o.reshape(1, *w.shape[1:]) for o, w in zip(outs[1:], riders)]


def _merge_kernel(x_ref, g_ref, wga_ref, wgb_ref, wgc_ref, ya_ref, yb_ref, yc_ref, wb_ref, wo_ref, o_ref):
    for rows in _sub_tiles(x_ref.shape[0]):
        x = x_ref[rows, :]
        h = _rms(x, g_ref[...]).astype(BF16)
        merged = None
        for n, (wg_ref, y_ref) in enumerate(((wga_ref, ya_ref), (wgb_ref, yb_ref), (wgc_ref, yc_ref))):
            gate = jax.nn.sigmoid(_dot(h, wg_ref[...]))
            term = gate * _dot(y_ref[rows, :], wb_ref[n])
            merged = term if merged is None else merged + term
        o_ref[rows, :] = x + _dot(merged.astype(BF16), wo_ref[...])


def _merge(x, g, w_in, ya, yb, yc, w_branch, w_out, layer, w_layer, name):
    rows, d = x.shape
    wd = ya.shape[1]
    tm = MERGE_TM
    first_gate_block = (w_in.shape[2] - N_BRANCHES * d) // d
    row_d = pl.BlockSpec((tm, d), lambda i: (i, 0))
    row_w = pl.BlockSpec((tm, wd), lambda i: (i, 0))
    gate_w = [pl.BlockSpec((None, d, d), lambda i, n=n: (w_layer, 0, first_gate_block + n),
                           pipeline_mode=pl.Buffered(1)) for n in range(N_BRANCHES)]
    return pl.pallas_call(
        _merge_kernel,
        out_shape=jax.ShapeDtypeStruct((rows, d), F32),
        grid=(rows // tm,),
        in_specs=[
            row_d,
            _layer_spec((1, d), layer),
            *gate_w,
            row_w, row_w, row_w,
            _layer_spec((N_BRANCHES, wd, d), w_layer),
            _layer_spec((d, d), w_layer),
        ],
        out_specs=row_d,
        compiler_params=_params("parallel"),
        name=name,
    )(x, g, w_in, w_in, w_in, ya, yb, yc, w_branch, w_out)


def _xattn_kernel(x_ref, g_ref, wq_ref, k_ref, v_ref, wo_ref, o_ref):
    d = x_ref.shape[1]
    dh = d // XA_HEADS
    for rows in _sub_tiles(x_ref.shape[0]):
        x = x_ref[rows, :]
        h = _rms(x, g_ref[...]).astype(BF16)
        q = (_dot(h, wq_ref[...]) * (dh ** -0.5)).astype(BF16)
        heads = []
        for hd in range(XA_HEADS):
            cols = slice(hd * dh, (hd + 1) * dh)
            s = _dot_nt(q[:, cols], k_ref[:, cols])
            p = jnp.exp(s - jnp.max(s, axis=-1, keepdims=True))
            p = p / jnp.sum(p, axis=-1, keepdims=True)
            heads.append(_dot(p.astype(BF16), v_ref[:, cols]).astype(BF16))
        o_ref[rows, :] = x + _dot(jnp.concatenate(heads, axis=1), wo_ref[...])


def _xattn(x, g, wq, k, v, wo, layer, w_layer, name):
    b, s, d = x.shape
    mem = k.shape[1]
    tm = XA_TM
    row_d = pl.BlockSpec((None, tm, d), lambda bi, i: (bi, i, 0))
    return pl.pallas_call(
        _xattn_kernel,
        out_shape=jax.ShapeDtypeStruct((b, s, d), F32),
        grid=(b, s // tm),
        in_specs=[
            row_d,
            _layer_spec((1, d), layer),
            _layer_spec((d, d), w_layer),
            pl.BlockSpec((None, mem, d), lambda bi, i: (bi, 0, 0)),
            pl.BlockSpec((None, mem, d), lambda bi, i: (bi, 0, 0)),
            _layer_spec((d, d), w_layer),
        ],
        out_specs=row_d,
        compiler_params=_params("parallel", "parallel"),
        name=name,
    )(x, g, wq, k, v, wo)


def _ffn_kernel(x_ref, g_ref, wg_ref, wu_ref, wd_ref, fg_ref, o_ref, h_ref, acc_ref, *, final_norm):
    subs = _sub_tiles(x_ref.shape[0])
    for rows in subs:
        h_ref[rows, :] = _rms(x_ref[rows, :], g_ref[...]).astype(BF16)
    for c in range(0, wg_ref.shape[1], FFN_TH):
        cols = slice(c, c + FFN_TH)
        for rows in subs:
            h = h_ref[rows, :]
            a = _dot(h, wg_ref[:, cols])
            t = ((a * jax.nn.sigmoid(a)) * _dot(h, wu_ref[:, cols])).astype(BF16)
            if c == 0:
                acc_ref[rows, :] = _dot(t, wd_ref[cols, :])
            else:
                acc_ref[rows, :] += _dot(t, wd_ref[cols, :])
    for rows in subs:
        y = x_ref[rows, :] + acc_ref[rows, :]
        if final_norm:
            y = _rms(y, fg_ref[...])
        o_ref[rows, :] = y


def _ffn(x, g, w_gate, w_up, w_down, final_g, final_norm, layer, w_layer, name):
    rows, d = x.shape
    hid = w_gate.shape[2]
    tm = FFN_TM
    row_d = pl.BlockSpec((tm, d), lambda i: (i, 0))
    return pl.pallas_call(
        functools.partial(_ffn_kernel, final_norm=final_norm),
        out_shape=jax.ShapeDtypeStruct((rows, d), F32),
        grid=(rows // tm,),
        in_specs=[
            row_d,
            _layer_spec((1, d), layer),
            _layer_spec((d, hid), w_layer),
            _layer_spec((d, hid), w_layer),
            _layer_spec((hid, d), w_layer),
            pl.BlockSpec((1, d), lambda i: (0, 0)),
        ],
        out_specs=row_d,
        scratch_shapes=[pltpu.VMEM((tm, d), BF16), pltpu.VMEM((tm, d), F32)],
        compiler_params=_params("parallel"),
        name=name,
    )(x, g, w_gate, w_up, w_down, final_g)


def kernel(x, mem, norm_mix_g, w_in, sgu_ln_g, sgu_ln_b, w_spatial, b_spatial, conv_w, w_branch, w_out, norm_xa_g, mem_norm_g, w_q_xa, w_k_xa, w_v_xa, w_o_xa, norm_ffn_g, w_gate_ffn, w_up_ffn, w_down_ffn, final_g):
    b, s, d = x.shape
    depth = w_in.shape[0]
    wd = w_branch.shape[2]
    n_mem = mem.shape[1]
    hid = w_gate_ffn.shape[2]
    assert d % LANES == 0 and s % max(ATT_TM, MIX_TM, MERGE_TM, XA_TM) == 0
    assert ATT_KG == ATT_TQ and ATT_KG % LANES == 0 and ATT_TM % ATT_TQ == 0 and SUB_TM % SGU_LEN == 0
    assert all(t % SUB_TM == 0 for t in (MIX_TM, MERGE_TM, XA_TM, FFN_TM))
    assert (b * s) % FFN_TM == 0 and wd == SB_HEADS * SB_HEAD_DIM
    assert hid % FFN_TH == 0 and w_in.shape[2] == 8 * wd + N_BRANCHES * d

    rows3 = lambda v: v.reshape(depth, 1, -1)
    g_mix, g_xa, g_mem, g_ffn = (rows3(g) for g in (norm_mix_g, norm_xa_g, mem_norm_g, norm_ffn_g))
    ln_g, ln_b = rows3(sgu_ln_g), rows3(sgu_ln_b)
    b_s_t = jnp.swapaxes(b_spatial, 1, 2)

    xf = x.reshape(b * s, d)
    memf = mem.reshape(b * n_mem, d)
    w_in_bf = w_in[:1].astype(BF16)
    for l in range(depth):
        qkv, yb, yc = _mix_in(xf, g_mix, w_in_bf, ln_g, ln_b, w_spatial, b_s_t, conv_w, s, l, 0, f"mix_in_{l}")
        riders = [w_gate_ffn, w_up_ffn, w_down_ffn, w_branch.reshape(depth, N_BRANCHES * wd, d), w_out,
                  w_q_xa, w_o_xa] + ([w_in] if l + 1 < depth else [])
        rider_layers = [l] * 7 + [l + 1]
        ya, cast = _sb_attention_riders(qkv.reshape(b, s, 3 * wd), riders, rider_layers, f"sb_attn_{l}")
        w_gate_bf, w_up_bf, w_down_bf, w_branch_bf, w_out_bf, w_q_bf, w_o_bf = cast[:7]
        xf = _merge(xf, g_mix, w_in_bf, ya.reshape(b * s, wd), yb, yc, w_branch_bf.reshape(1, N_BRANCHES, wd, d),
                    w_out_bf, l, 0, f"merge_{l}")
        k_mem, v_mem = (t.reshape(b, n_mem, d) for t in _mem_kv(memf, g_mem, w_k_xa, w_v_xa, l, f"mem_kv_{l}"))
        xf = _xattn(xf.reshape(b, s, d), g_xa, w_q_bf, k_mem, v_mem, w_o_bf, l, 0, f"xattn_{l}").reshape(b * s, d)
        xf = _ffn(xf, g_ffn, w_gate_bf, w_up_bf, w_down_bf, final_g.reshape(1, d), l == depth - 1, l, 0, f"ffn_{l}")
        if l + 1 < depth:
            w_in_bf = cast[7]
    return xf.reshape(b, s, d)
```

```python
import functools

import jax
import jax.numpy as jnp
from jax import lax
from jax.experimental import pallas as pl
from jax.experimental.pallas import tpu as pltpu

F32 = jnp.float32
BF16 = jnp.bfloat16

LANES = 128
V7X_VMEM_LIMIT_BYTES = 56 * 1024 * 1024

SB_HEADS = 8
SB_HEAD_DIM = 64
SGU_LEN = 128
SGU_GROUPS = 4
CHUNK = 64
CONV_WIDTH = 3
N_BRANCHES = 3
XA_HEADS = 4
RMS_EPS = 1e-6
LN_EPS = 1e-5
LOG2E = 1.4426950408889634
MASKED_LOG_WEIGHT = -1e30
DEAD_LOG_WEIGHT = -120.0

PROJ_TN = 512
SUB_TM = 512
MIX_TM = 1024
ATT_TQ = 256
ATT_TM = 1024
ATT_KG = 256
MERGE_TM = 1024
XA_TM = 2048
FFN_TM = 1024
FFN_TH = 256
CONV_HALO = 8


def _params(*semantics):
    return pltpu.CompilerParams(dimension_semantics=semantics,
                                vmem_limit_bytes=V7X_VMEM_LIMIT_BYTES)


def _layer_spec(shape, layer):
    zeros = (0,) * len(shape)
    return pl.BlockSpec((None, *shape), lambda *_: (layer, *zeros), pipeline_mode=pl.Buffered(1))


def _sub_tiles(n_rows):
    return [slice(r, r + SUB_TM) for r in range(0, n_rows, SUB_TM)]


def _rms(x, g):
    return x * lax.rsqrt(jnp.mean(x * x, axis=-1, keepdims=True) + RMS_EPS) * g


def _dot(a, b):
    return jnp.dot(a, b, preferred_element_type=F32)


def _dot_nt(a, b):
    return lax.dot_general(a, b, (((1,), (1,)), ((), ())), preferred_element_type=F32)


def _mem_kv_kernel(x_ref, g_ref, wk_ref, wv_ref, k_ref, v_ref, h_ref):
    @pl.when(pl.program_id(1) == 0)
    def _():
        h_ref[...] = _rms(x_ref[...], g_ref[...]).astype(BF16)

    h = h_ref[...]
    k_ref[...] = _dot(h, wk_ref[...].astype(BF16)).astype(k_ref.dtype)
    v_ref[...] = _dot(h, wv_ref[...].astype(BF16)).astype(v_ref.dtype)


def _mem_kv(x, g, wk, wv, name):
    rows, d = x.shape
    depth, _, cols = wk.shape
    w_spec = pl.BlockSpec((None, d, PROJ_TN), lambda l, j: (l, 0, j))
    o_spec = pl.BlockSpec((None, rows, PROJ_TN), lambda l, j: (l, 0, j))
    out = jax.ShapeDtypeStruct((depth, rows, cols), BF16)
    return pl.pallas_call(
        _mem_kv_kernel,
        out_shape=(out, out),
        grid=(depth, cols // PROJ_TN),
        in_specs=[
            pl.BlockSpec((rows, d), lambda l, j: (0, 0)),
            pl.BlockSpec((None, 1, d), lambda l, j: (l, 0, 0)),
            w_spec, w_spec,
        ],
        out_specs=(o_spec, o_spec),
        scratch_shapes=[pltpu.VMEM((rows, d), BF16)],
        compiler_params=_params("arbitrary", "arbitrary"),
        name=name,
    )(x, g, wk, wv)


def _spatial_gating(z, lng_ref, lnb_ref, ws_ref, bs_ref, o_ref):
    tm = z.shape[0]
    wd = o_ref.shape[1]
    gd = wd // SGU_GROUPS
    z = 0.5 * z * (1.0 + lax.erf(z * (2.0 ** -0.5)))
    u = z[:, :wd]
    v = z[:, wd:]
    mu = jnp.mean(v, axis=-1, keepdims=True)
    vc = v - mu
    v = vc * lax.rsqrt(jnp.mean(vc * vc, axis=-1, keepdims=True) + LN_EPS)
    v = (v * lng_ref[...] + lnb_ref[...]).astype(BF16)
    pos_t = lax.broadcasted_iota(jnp.int32, (SGU_LEN, SGU_LEN), 0)
    pos_s = lax.broadcasted_iota(jnp.int32, (SGU_LEN, SGU_LEN), 1)
    allowed = (pos_t // CHUNK) >= (pos_s // CHUNK)
    for gi in range(SGU_GROUPS):
        wm = jnp.where(allowed, ws_ref[gi], 0.0).astype(BF16)
        bias = bs_ref[:, gi:gi + 1]
        for c in range(tm // SGU_LEN):
            rows = slice(c * SGU_LEN, (c + 1) * SGU_LEN)
            cols = slice(gi * gd, (gi + 1) * gd)
            vm = _dot(wm, v[rows, cols]) + bias
            o_ref[rows, cols] = (u[rows, cols] * vm).astype(o_ref.dtype)


def _gated_conv(p, cw_ref, o_ref, halo):
    tm = p.shape[0]
    wd = o_ref.shape[1]
    cb = p[:, :wd]
    y = p[:, wd:2 * wd] * p[:, 2 * wd:]
    prev1 = halo[CONV_HALO - 1:CONV_HALO, :]
    prev2 = halo[CONV_HALO - 2:CONV_HALO - 1, :]
    row = lax.broadcasted_iota(jnp.int32, (tm, wd), 0)
    y1 = jnp.where(row == 0, prev1, pltpu.roll(y, 1, axis=0))
    y2 = jnp.where(row == 0, prev2, jnp.where(row == 1, prev1, pltpu.roll(y, 2, axis=0)))
    cw = cw_ref[...]
    conv = cw[0:1, :] * y2 + cw[1:2, :] * y1 + cw[2:3, :] * y
    o_ref[...] = (cb * conv).astype(o_ref.dtype)
    return y[tm - CONV_HALO:, :]


def _mix_in_kernel(x_ref, g_ref, w_ref, lng_ref, lnb_ref, ws_ref, bs_ref, cw_ref,
                   qkv_ref, yb_ref, yc_ref, halo_ref, *, tiles_per_seq):
    wd = yb_ref.shape[1]
    c_qkv, c_sgu = 3 * wd, 5 * wd
    halo = jnp.where(pl.program_id(0) % tiles_per_seq == 0, 0.0, halo_ref[...])
    subs = _sub_tiles(x_ref.shape[0])
    proj = []
    for rows in subs:
        h = _rms(x_ref[rows, :], g_ref[...]).astype(BF16)
        qkv_ref[rows, :] = _dot(h, w_ref[:, :c_qkv]).astype(qkv_ref.dtype)
        proj.append((_dot(h, w_ref[:, c_qkv:c_sgu]), _dot(h, w_ref[:, c_sgu:])))
    for rows, (z, p) in zip(subs, proj):
        _spatial_gating(z, lng_ref, lnb_ref, ws_ref, bs_ref, yb_ref.at[rows, :])
        halo = _gated_conv(p, cw_ref, yc_ref.at[rows, :], halo)
    halo_ref[...] = halo


def _mix_in(x, g, w_in, ln_g, ln_b, w_s, b_s_t, conv_w, seq, layer, w_layer, name):
    rows, d = x.shape
    wd = ln_g.shape[2]
    tm = MIX_TM
    row_w = pl.BlockSpec((tm, wd), lambda i: (i, 0))
    return pl.pallas_call(
        functools.partial(_mix_in_kernel, tiles_per_seq=seq // tm),
        out_shape=(jax.ShapeDtypeStruct((rows, 3 * wd), BF16),
                   jax.ShapeDtypeStruct((rows, wd), BF16),
                   jax.ShapeDtypeStruct((rows, wd), BF16)),
        grid=(rows // tm,),
        in_specs=[
            pl.BlockSpec((tm, d), lambda i: (i, 0)),
            _layer_spec((1, d), layer),
            _layer_spec((d, 8 * wd), w_layer),
            _layer_spec((1, wd), layer),
            _layer_spec((1, wd), layer),
            _layer_spec((SGU_GROUPS, SGU_LEN, SGU_LEN), layer),
            _layer_spec((SGU_LEN, SGU_GROUPS), layer),
            _layer_spec((CONV_WIDTH, wd), layer),
        ],
        out_specs=(pl.BlockSpec((tm, 3 * wd), lambda i: (i, 0)), row_w, row_w),
        scratch_shapes=[pltpu.VMEM((CONV_HALO, wd), F32)],
        compiler_params=_params("arbitrary"),
        name=name,
    )(x, g, w_in, ln_g, ln_b, w_s, b_s_t, conv_w)


def _sb_attn_kernel(q_ref, k_ref, v_ref, tri_ref, *refs, n_riders):
    rider_in, o_ref, rider_out = refs[:n_riders], refs[n_riders], refs[n_riders + 1:2 * n_riders + 1]
    acc_ref, carry_ref = refs[2 * n_riders + 1:]
    for src, dst in zip(rider_in, rider_out):
        dst[...] = src[...].astype(BF16)
    tq, kg = ATT_TQ, ATT_KG
    n_pairs = q_ref.shape[1] // LANES
    n_qb = q_ref.shape[0] // tq
    first_qb = pl.program_id(1) * n_qb
    lane = lax.broadcasted_iota(jnp.int32, (tq, LANES), 1)
    row = lax.broadcasted_iota(jnp.int32, (2 * tq, kg), 0)
    col = lax.broadcasted_iota(jnp.int32, (2 * tq, kg), 1)
    diag_bias = jnp.where(col < jnp.where(row >= tq, row - tq, row), 0.0, MASKED_LOG_WEIGHT)

    def stacked_queries(qb, p):
        q = q_ref[qb * tq:(qb + 1) * tq, p * LANES:(p + 1) * LANES] * jnp.asarray(SB_HEAD_DIM ** -0.5, BF16)
        zero = jnp.zeros_like(q)
        return jnp.concatenate([jnp.where(lane < SB_HEAD_DIM, q, zero),
                                jnp.where(lane >= SB_HEAD_DIM, q, zero)], axis=0)

    def group(qb, p, g, bias=None, first=False):
        keys = pl.ds(pl.multiple_of(g * kg, kg), kg)
        z = _dot_nt(q2s[qb][p], k_ref[keys, p * LANES:(p + 1) * LANES])
        if bias is not None:
            z = z + bias
        sp = jnp.log(1.0 + jnp.exp2(jnp.abs(z) * -LOG2E))
        log_sig = jnp.minimum(z, 0.0) - sp
        log_1m = log_sig - z
        suffix = _dot(log_1m.astype(BF16), tri_ref[...])
        total = jnp.sum(log_1m, axis=-1, keepdims=True)
        vs = v_ref[keys, p * LANES:(p + 1) * LANES]
        if first:
            acc_ref[qb, p] = _dot(jnp.exp(log_sig + suffix).astype(BF16), vs)
            carry_ref[qb, p] = total
        else:
            carry = carry_ref[qb, p]
            acc_ref[qb, p] += _dot(jnp.exp(log_sig + suffix + carry).astype(BF16), vs)
            carry_ref[qb, p] = carry + total

    q2s = [[stacked_queries(qb, p) for p in range(n_pairs)] for qb in range(n_qb)]
    for qb in range(n_qb):
        for p in range(n_pairs):
            group(qb, p, first_qb + qb, diag_bias, first=True)
    for qb in range(n_qb):
        i = first_qb + qb
        bias = None if qb > 0 else jnp.where(i >= 1, 0.0, MASKED_LOG_WEIGHT)
        for p in range(n_pairs):
            group(qb, p, jnp.maximum(i - 1, 0), bias)

    @pl.when(jnp.logical_and(first_qb + n_qb - 1 >= 2, jnp.max(carry_ref[...]) > DEAD_LOG_WEIGHT))
    def _():
        for qb in range(n_qb):
            for p in range(n_pairs):
                def step(g, qb=qb, p=p):
                    group(qb, p, g)
                    return g - 1

                lax.while_loop(
                    lambda g, qb=qb, p=p: jnp.logical_and(g >= 0, jnp.max(carry_ref[qb, p]) > DEAD_LOG_WEIGHT),
                    step, first_qb + qb - 2)

    for qb in range(n_qb):
        for p in range(n_pairs):
            acc = acc_ref[qb, p]
            o_ref[qb * tq:(qb + 1) * tq, p * LANES:(p + 1) * LANES] = jnp.where(
                lane < SB_HEAD_DIM, acc[:tq], acc[tq:]).astype(o_ref.dtype)


def _tri_matrix():
    j = lax.broadcasted_iota(jnp.int32, (ATT_KG, ATT_KG), 0)
    s = lax.broadcasted_iota(jnp.int32, (ATT_KG, ATT_KG), 1)
    return (j > s).astype(BF16)


def _sb_attention_riders(qkv, riders, layers, name):
    b, s, _ = qkv.shape
    tq = ATT_TM
    width = SB_HEADS * SB_HEAD_DIM
    pairs = width // LANES
    steps_per_batch = s // tq
    n_steps = b * steps_per_batch
    split = [w.reshape(w.shape[0], n_steps, w.shape[1] // n_steps, w.shape[2]) for w in riders]
    rider_specs = [pl.BlockSpec((None, None, *w.shape[2:]), lambda bi, i, la=la: (la, bi * steps_per_batch + i, 0, 0))
                   for w, la in zip(split, layers)]
    rider_out_specs = [pl.BlockSpec((None, *w.shape[2:]), lambda bi, i: (bi * steps_per_batch + i, 0, 0))
                       for w in split]
    outs = pl.pallas_call(
        functools.partial(_sb_attn_kernel, n_riders=len(riders)),
        out_shape=(jax.ShapeDtypeStruct((b, s, width), BF16),
                   *[jax.ShapeDtypeStruct(w.shape[1:], BF16) for w in split]),
        grid=(b, steps_per_batch),
        in_specs=[
            pl.BlockSpec((None, tq, width), lambda bi, i: (bi, i, 0)),
            pl.BlockSpec((None, s, width), lambda bi, i: (bi, 0, 1)),
            pl.BlockSpec((None, s, width), lambda bi, i: (bi, 0, 2)),
            pl.BlockSpec((ATT_KG, ATT_KG), lambda bi, i: (0, 0)),
            *rider_specs,
        ],
        out_specs=(pl.BlockSpec((None, tq, width), lambda bi, i: (bi, i, 0)), *rider_out_specs),
        scratch_shapes=[
            pltpu.VMEM((tq // ATT_TQ, pairs, 2 * ATT_TQ, LANES), F32),
            pltpu.VMEM((tq // ATT_TQ, pairs, 2 * ATT_TQ, 1), F32),
        ],
        compiler_params=_params("parallel", "arbitrary"),
        name=name,
    )(qkv, qkv, qkv, _tri_matrix(), *split)
    return outs[0], [o.reshape(1, *w.shape[1:]) for o, w in zip(outs[1:], riders)]


def _merge_kernel(x_ref, g_ref, wga_ref, wgb_ref, wgc_ref, ya_ref, yb_ref, yc_ref, wb_ref, wo_ref, o_ref):
    for rows in _sub_tiles(x_ref.shape[0]):
        x = x_ref[rows, :]
        h = _rms(x, g_ref[...]).astype(BF16)
        merged = None
        for n, (wg_ref, y_ref) in enumerate(((wga_ref, ya_ref), (wgb_ref, yb_ref), (wgc_ref, yc_ref))):
            gate = jax.nn.sigmoid(_dot(h, wg_ref[...]))
            term = gate * _dot(y_ref[rows, :], wb_ref[n])
            merged = term if merged is None else merged + term
        o_ref[rows, :] = x + _dot(merged.astype(BF16), wo_ref[...])


def _merge(x, g, w_in, ya, yb, yc, w_branch, w_out, layer, w_layer, name):
    rows, d = x.shape
    wd = ya.shape[1]
    tm = MERGE_TM
    first_gate_block = (w_in.shape[2] - N_BRANCHES * d) // d
    row_d = pl.BlockSpec((tm, d), lambda i: (i, 0))
    row_w = pl.BlockSpec((tm, wd), lambda i: (i, 0))
    gate_w = [pl.BlockSpec((None, d, d), lambda i, n=n: (w_layer, 0, first_gate_block + n),
                           pipeline_mode=pl.Buffered(1)) for n in range(N_BRANCHES)]
    return pl.pallas_call(
        _merge_kernel,
        out_shape=jax.ShapeDtypeStruct((rows, d), F32),
        grid=(rows // tm,),
        in_specs=[
            row_d,
            _layer_spec((1, d), layer),
            *gate_w,
            row_w, row_w, row_w,
            _layer_spec((N_BRANCHES, wd, d), w_layer),
            _layer_spec((d, d), w_layer),
        ],
        out_specs=row_d,
        compiler_params=_params("parallel"),
        name=name,
    )(x, g, w_in, w_in, w_in, ya, yb, yc, w_branch, w_out)


def _xattn_kernel(x_ref, g_ref, wq_ref, k_ref, v_ref, wo_ref, o_ref):
    d = x_ref.shape[1]
    dh = d // XA_HEADS
    for rows in _sub_tiles(x_ref.shape[0]):
        x = x_ref[rows, :]
        h = _rms(x, g_ref[...]).astype(BF16)
        q = (_dot(h, wq_ref[...]) * (dh ** -0.5)).astype(BF16)
        heads = []
        for hd in range(XA_HEADS):
            cols = slice(hd * dh, (hd + 1) * dh)
            s = _dot_nt(q[:, cols], k_ref[:, cols])
            p = jnp.exp(s - jnp.max(s, axis=-1, keepdims=True))
            p = p / jnp.sum(p, axis=-1, keepdims=True)
            heads.append(_dot(p.astype(BF16), v_ref[:, cols]).astype(BF16))
        o_ref[rows, :] = x + _dot(jnp.concatenate(heads, axis=1), wo_ref[...])


def _xattn(x, g, wq, k, v, wo, layer, w_layer, name):
    b, s, d = x.shape
    mem = k.shape[2]
    tm = XA_TM
    row_d = pl.BlockSpec((None, tm, d), lambda bi, i: (bi, i, 0))
    return pl.pallas_call(
        _xattn_kernel,
        out_shape=jax.ShapeDtypeStruct((b, s, d), F32),
        grid=(b, s // tm),
        in_specs=[
            row_d,
            _layer_spec((1, d), layer),
            _layer_spec((d, d), w_layer),
            pl.BlockSpec((None, None, mem, d), lambda bi, i: (layer, bi, 0, 0)),
            pl.BlockSpec((None, None, mem, d), lambda bi, i: (layer, bi, 0, 0)),
            _layer_spec((d, d), w_layer),
        ],
        out_specs=row_d,
        compiler_params=_params("parallel", "parallel"),
        name=name,
    )(x, g, wq, k, v, wo)


def _ffn_kernel(x_ref, g_ref, wg_ref, wu_ref, wd_ref, fg_ref, o_ref, h_ref, acc_ref, *, final_norm):
    subs = _sub_tiles(x_ref.shape[0])
    for rows in subs:
        h_ref[rows, :] = _rms(x_ref[rows, :], g_ref[...]).astype(BF16)
    for c in range(0, wg_ref.shape[1], FFN_TH):
        cols = slice(c, c + FFN_TH)
        for rows in subs:
            h = h_ref[rows, :]
            a = _dot(h, wg_ref[:, cols])
            t = ((a * jax.nn.sigmoid(a)) * _dot(h, wu_ref[:, cols])).astype(BF16)
            if c == 0:
                acc_ref[rows, :] = _dot(t, wd_ref[cols, :])
            else:
                acc_ref[rows, :] += _dot(t, wd_ref[cols, :])
    for rows in subs:
        y = x_ref[rows, :] + acc_ref[rows, :]
        if final_norm:
            y = _rms(y, fg_ref[...])
        o_ref[rows, :] = y


def _ffn(x, g, w_gate, w_up, w_down, final_g, final_norm, layer, w_layer, name):
    rows, d = x.shape
    hid = w_gate.shape[2]
    tm = FFN_TM
    row_d = pl.BlockSpec((tm, d), lambda i: (i, 0))
    return pl.pallas_call(
        functools.partial(_ffn_kernel, final_norm=final_norm),
        out_shape=jax.ShapeDtypeStruct((rows, d), F32),
        grid=(rows // tm,),
        in_specs=[
            row_d,
            _layer_spec((1, d), layer),
            _layer_spec((d, hid), w_layer),
            _layer_spec((d, hid), w_layer),
            _layer_spec((hid, d), w_layer),
            pl.BlockSpec((1, d), lambda i: (0, 0)),
        ],
        out_specs=row_d,
        scratch_shapes=[pltpu.VMEM((tm, d), BF16), pltpu.VMEM((tm, d), F32)],
        compiler_params=_params("parallel"),
        name=name,
    )(x, g, w_gate, w_up, w_down, final_g)


def kernel(x, mem, norm_mix_g, w_in, sgu_ln_g, sgu_ln_b, w_spatial, b_spatial, conv_w, w_branch, w_out, norm_xa_g, mem_norm_g, w_q_xa, w_k_xa, w_v_xa, w_o_xa, norm_ffn_g, w_gate_ffn, w_up_ffn, w_down_ffn, final_g):
    b, s, d = x.shape
    depth = w_in.shape[0]
    wd = w_branch.shape[2]
    n_mem = mem.shape[1]
    hid = w_gate_ffn.shape[2]
    assert d % LANES == 0 and s % max(ATT_TM, MIX_TM, MERGE_TM, XA_TM) == 0
    assert ATT_KG == ATT_TQ and ATT_KG % LANES == 0 and ATT_TM % ATT_TQ == 0 and SUB_TM % SGU_LEN == 0
    assert all(t % SUB_TM == 0 for t in (MIX_TM, MERGE_TM, XA_TM, FFN_TM))
    assert (b * s) % FFN_TM == 0 and wd == SB_HEADS * SB_HEAD_DIM
    assert hid % FFN_TH == 0 and w_in.shape[2] == 8 * wd + N_BRANCHES * d

    rows3 = lambda v: v.reshape(depth, 1, -1)
    g_mix, g_xa, g_mem, g_ffn = (rows3(g) for g in (norm_mix_g, norm_xa_g, mem_norm_g, norm_ffn_g))
    ln_g, ln_b = rows3(sgu_ln_g), rows3(sgu_ln_b)
    b_s_t = jnp.swapaxes(b_spatial, 1, 2)

    xf = x.reshape(b * s, d)
    k_mem, v_mem = (t.reshape(depth, b, n_mem, d)
                    for t in _mem_kv(mem.reshape(b * n_mem, d), g_mem, w_k_xa, w_v_xa, "mem_kv"))
    w_in_bf = w_in[:1].astype(BF16)
    for l in range(depth):
        qkv, yb, yc = _mix_in(xf, g_mix, w_in_bf, ln_g, ln_b, w_spatial, b_s_t, conv_w, s, l, 0, f"mix_in_{l}")
        riders = [w_gate_ffn, w_up_ffn, w_down_ffn, w_branch.reshape(depth, N_BRANCHES * wd, d), w_out,
                  w_q_xa, w_o_xa] + ([w_in] if l + 1 < depth else [])
        rider_layers = [l] * 7 + [l + 1]
        ya, cast = _sb_attention_riders(qkv.reshape(b, s, 3 * wd), riders, rider_layers, f"sb_attn_{l}")
        w_gate_bf, w_up_bf, w_down_bf, w_branch_bf, w_out_bf, w_q_bf, w_o_bf = cast[:7]
        xf = _merge(xf, g_mix, w_in_bf, ya.reshape(b * s, wd), yb, yc, w_branch_bf.reshape(1, N_BRANCHES, wd, d),
                    w_out_bf, l, 0, f"merge_{l}")
        xf = _xattn(xf.reshape(b, s, d), g_xa, w_q_bf, k_mem, v_mem, w_o_bf, l, 0, f"xattn_{l}").reshape(b * s, d)
        xf = _ffn(xf, g_ffn, w_gate_bf, w_up_bf, w_down_bf, final_g.reshape(1, d), l == depth - 1, l, 0, f"ffn_{l}")
        if l + 1 < depth:
            w_in_bf = cast[7]
    return xf.reshape(b, s, d)
```
